```python
import math
import jax, jax.numpy as jnp
from jax import lax
import numpy as np

D_MODEL = 2048
BATCH = 1
SEQ = 16384
DEPTH = 4

GRID_W = 64
CTX_LEN = 256
A_HEADS = 8
A_QK_DIM = 64
A_V_DIM = 2 * A_QK_DIM
A_WIDTH = A_HEADS * A_V_DIM
ROPE_FREQS = A_QK_DIM // 4
ROPE_BASE = 10000.0
Q_BLOCK = 128
B_GROUPS = 8
B_CHUNK = 128
B_GROUP_DIM = 128
B_WIDTH = B_GROUPS * B_GROUP_DIM
NORM_EPS = 1e-6
IN_SIZES = (A_HEADS * 2 * A_QK_DIM, A_HEADS * 2 * A_QK_DIM, A_WIDTH, A_WIDTH,
            B_WIDTH, B_WIDTH, B_WIDTH, 2 * D_MODEL)
IN_WIDTH = sum(IN_SIZES)

kernel_name = "hybrid_diffattn_chunkgmlp_prefix_dit"


def _rmsnorm(x, g):
    xf = x.astype(jnp.float32)
    y = xf * lax.rsqrt(jnp.mean(xf * xf, axis=-1, keepdims=True) + NORM_EPS)
    return (y * g.astype(jnp.float32)).astype(x.dtype)


def _lambda_init(layer):
    return 0.8 - 0.6 * math.exp(-0.3 * layer)


def _axial_rope_tables(row, col, dtype):
    inv = ROPE_BASE ** (-jnp.arange(ROPE_FREQS, dtype=jnp.float32) / ROPE_FREQS)
    ang = jnp.stack([row.astype(jnp.float32)[:, None] * inv,
                     col.astype(jnp.float32)[:, None] * inv], axis=1)
    return jnp.cos(ang).astype(dtype), jnp.sin(ang).astype(dtype)


def _apply_rope(x, cos, sin):
    b, t, h, m, _ = x.shape
    xr = x.reshape(b, t, h, m, 2, 2, ROPE_FREQS)
    x1, x2 = xr[..., 0, :], xr[..., 1, :]
    cs = cos[None, :, None, None]
    sn = sin[None, :, None, None]
    out = jnp.stack([x1 * cs - x2 * sn, x2 * cs + x1 * sn], axis=-2)
    return out.reshape(x.shape)


def _project(h, shift, scale, norm_g, w_in, q_norm_g, k_norm_g):
    hn = _rmsnorm(h, norm_g) * (1.0 + scale) + shift
    z = hn @ w_in
    cuts, acc = [], 0
    for s in IN_SIZES[:-1]:
        acc += s
        cuts.append(acc)
    q, k, v, ga, ub, vb, gb, mg = jnp.split(z, cuts, axis=-1)
    b, t = h.shape[0], h.shape[1]
    q = _rmsnorm(q.reshape(b, t, A_HEADS, 2, A_QK_DIM), q_norm_g)
    k = _rmsnorm(k.reshape(b, t, A_HEADS, 2, A_QK_DIM), k_norm_g)
    v = v.reshape(b, t, A_HEADS, A_V_DIM)
    return q, k, v, ga, ub, vb, gb, mg


def _pair_attend(q, k, v):
    s = jnp.einsum('bqhmd,bkhmd->bhmqk', q, k).astype(jnp.float32) * (A_QK_DIM ** -0.5)
    p = jax.nn.softmax(s, axis=-1).astype(v.dtype)
    return jnp.einsum('bhmqk,bkhe->bqhme', p, v)


def _latent_attend(q, k_all, v_all):
    b, t = q.shape[0], q.shape[1]
    qb = q.reshape(b, t // Q_BLOCK, Q_BLOCK, A_HEADS, 2, A_QK_DIM).swapaxes(0, 1)
    o = lax.map(lambda qi: _pair_attend(qi, k_all, v_all), qb)
    return o.swapaxes(0, 1).reshape(b, t, A_HEADS, 2, A_V_DIM)


def _diff_heads(o, lam, lam_init, subln_g):
    d = o[..., 0, :] - lam.astype(o.dtype) * o[..., 1, :]
    d = _rmsnorm(d, subln_g) * (1.0 - lam_init)
    return d.reshape(d.shape[0], d.shape[1], A_WIDTH)


def _chunk_gmlp(ub, vb, v_norm_g, w_spatial, b_spatial):
    b, t, _ = ub.shape
    u = jax.nn.gelu(ub)
    v = _rmsnorm(jax.nn.gelu(vb).reshape(b, t // B_CHUNK, B_CHUNK, B_GROUPS, B_GROUP_DIM), v_norm_g)
    s = jnp.einsum('gpq,bnqgc->bnpgc', w_spatial, v) + b_spatial.T[None, None, :, :, None]
    return u * s.reshape(b, t, B_WIDTH)


def _merge(a, ga, bo, gb, mg, w_proj_a, w_proj_b, w_out):
    ya = (a * jax.nn.silu(ga)) @ w_proj_a
    yb = (bo * jax.nn.silu(gb)) @ w_proj_b
    ma, mb = jnp.split(jax.nn.sigmoid(mg), 2, axis=-1)
    return (ma * ya + mb * yb) @ w_out


def setup_inputs(seed: int = 0) -> dict:
    key = jax.random.key(seed)
    ks = jax.random.split(key, 24)
    f32 = jnp.float32

    def nrm(k, shape, s):
        return jax.random.normal(k, shape, f32) * s

    return {
        "x": nrm(ks[0], (BATCH, SEQ, D_MODEL), 1.0),
        "c": nrm(ks[1], (BATCH, D_MODEL), 1.0),
        "ctx": nrm(ks[2], (BATCH, CTX_LEN, D_MODEL), 1.0),
        "c_ctx": nrm(ks[3], (D_MODEL,), 1.0),
        "w_ada": nrm(ks[4], (DEPTH, D_MODEL, 3 * D_MODEL), D_MODEL ** -0.5),
        "b_ada": nrm(ks[5], (DEPTH, 3 * D_MODEL), 0.02),
        "norm_g": 1.0 + nrm(ks[6], (DEPTH, D_MODEL), 0.05),
        "w_in": nrm(ks[7], (DEPTH, D_MODEL, IN_WIDTH), D_MODEL ** -0.5),
        "q_norm_g": 1.0 + nrm(ks[8], (DEPTH, A_QK_DIM), 0.05),
        "k_norm_g": 1.0 + nrm(ks[9], (DEPTH, A_QK_DIM), 0.05),
        "lam_q1": nrm(ks[10], (DEPTH, A_QK_DIM), 0.1),
        "lam_k1": nrm(ks[11], (DEPTH, A_QK_DIM), 0.1),
        "lam_q2": nrm(ks[12], (DEPTH, A_QK_DIM), 0.1),
        "lam_k2": nrm(ks[13], (DEPTH, A_QK_DIM), 0.1),
        "subln_g": 1.0 + nrm(ks[14], (DEPTH, A_V_DIM), 0.05),
        "v_norm_g": 1.0 + nrm(ks[15], (DEPTH, B_GROUPS, B_GROUP_DIM), 0.05),
        "w_spatial": nrm(ks[16], (DEPTH, B_GROUPS, B_CHUNK, B_CHUNK), 0.5 * B_CHUNK ** -0.5),
        "b_spatial": 1.0 + nrm(ks[17], (DEPTH, B_GROUPS, B_CHUNK), 0.1),
        "w_proj_a": nrm(ks[18], (DEPTH, A_WIDTH, D_MODEL), A_WIDTH ** -0.5),
        "w_proj_b": nrm(ks[19], (DEPTH, B_WIDTH, D_MODEL), B_WIDTH ** -0.5),
        "w_out": nrm(ks[20], (DEPTH, D_MODEL, D_MODEL), D_MODEL ** -0.5),
    }


def reference(x, c, ctx, c_ctx, w_ada, b_ada, norm_g, w_in, q_norm_g, k_norm_g,
              lam_q1, lam_k1, lam_q2, lam_k2, subln_g, v_norm_g, w_spatial, b_spatial,
              w_proj_a, w_proj_b, w_out):
    seq = x.shape[1]
    rows = seq // GRID_W
    row = jnp.repeat(jnp.arange(rows, dtype=jnp.int32), GRID_W)
    col = jnp.tile(jnp.arange(GRID_W, dtype=jnp.int32), rows)
    cos, sin = _axial_rope_tables(row, col, x.dtype)

    silu_c = jax.nn.silu(c)
    silu_cc = jax.nn.silu(c_ctx)
    h, hc = x, ctx
    for layer in range(DEPTH):
        mod = silu_c @ w_ada[layer] + b_ada[layer]
        shift, scale, gate = [m[:, None, :] for m in jnp.split(mod, 3, axis=-1)]
        mod_c = silu_cc @ w_ada[layer] + b_ada[layer]
        shift_c, scale_c, gate_c = jnp.split(mod_c, 3)
        lam_init = _lambda_init(layer)
        lam = (jnp.exp(jnp.sum(lam_q1[layer] * lam_k1[layer]).astype(jnp.float32))
               - jnp.exp(jnp.sum(lam_q2[layer] * lam_k2[layer]).astype(jnp.float32))
               + lam_init)

        qc, kc, vc, gac, ubc, vbc, gbc, mgc = _project(
            hc, shift_c, scale_c, norm_g[layer], w_in[layer], q_norm_g[layer], k_norm_g[layer])
        q, k, v, ga, ub, vb, gb, mg = _project(
            h, shift, scale, norm_g[layer], w_in[layer], q_norm_g[layer], k_norm_g[layer])
        q = _apply_rope(q, cos, sin)
        k = _apply_rope(k, cos, sin)

        k_all = jnp.concatenate([kc, k], axis=1)
        v_all = jnp.concatenate([vc, v], axis=1)
        a_lat = _diff_heads(_latent_attend(q, k_all, v_all), lam, lam_init, subln_g[layer])
        b_lat = _chunk_gmlp(ub, vb, v_norm_g[layer], w_spatial[layer], b_spatial[layer])
        out = _merge(a_lat, ga, b_lat, gb, mg, w_proj_a[layer], w_proj_b[layer], w_out[layer])
        h = h + gate * out

        if layer < DEPTH - 1:
            a_ctx = _diff_heads(_pair_attend(qc, kc, vc), lam, lam_init, subln_g[layer])
            b_ctx = _chunk_gmlp(ubc, vbc, v_norm_g[layer], w_spatial[layer], b_spatial[layer])
            out_c = _merge(a_ctx, gac, b_ctx, gbc, mgc, w_proj_a[layer], w_proj_b[layer], w_out[layer])
            hc = hc + gate_c * out_c
    return h
```

```python
import functools
import math

import jax
import jax.numpy as jnp
from jax import lax
from jax.experimental import pallas as pl
from jax.experimental.pallas import tpu as pltpu

D = 2048
DEPTH = 4
GRID_W = 64
CTX = 256
HEADS = 8
QK = 64
DV = 128
AW = HEADS * DV
ROPE_F = 16
ROPE_BASE = 10000.0
GROUPS = 8
CHUNK = 128
GDIM = 128
BW = GROUPS * GDIM
EPS = 1e-6
LOG2E = 1.4426950408889634

V7X_LANES = 128
V7X_VMEM_BYTES = 64 * 1024 * 1024

ZC_Q, ZC_V, ZC_GA, ZC_U, ZC_VN, ZC_GB, ZC_MG = 0, 1024, 2048, 3072, 4096, 5120, 6144
ZW = 6144 + 2 * D

TM = 1280
TN = 512
TQ = 256
TB = 256
TP = 640


def _sigmoid(x):
    return 1.0 / (1.0 + jnp.exp(-x))


def _gelu_tanh(x):
    return 0.5 * x * (1.0 + jnp.tanh(math.sqrt(2.0 / math.pi) * (x + 0.044715 * (x * x * x))))


def _mod_kernel(c_ref, w_ref, b_ref, o_ref):
    c = c_ref[...]
    s = c * _sigmoid(c)
    w = w_ref[0]
    b = b_ref[0]
    r0 = jnp.sum(w * s[:, 0:1], axis=0, keepdims=True) + b
    r1 = jnp.sum(w * s[:, 1:2], axis=0, keepdims=True) + b
    o_ref[0] = jnp.concatenate([r0, r1], axis=0)


def _mod_call(c2, w_ada, b_ada):
    tn = 512
    n = w_ada.shape[-1]
    return pl.pallas_call(
        _mod_kernel,
        grid=(DEPTH, n // tn),
        in_specs=[
            pl.BlockSpec((D, 2), lambda l, j: (0, 0)),
            pl.BlockSpec((1, D, tn), lambda l, j: (l, 0, j)),
            pl.BlockSpec((1, 1, tn), lambda l, j: (l, 0, j)),
        ],
        out_specs=pl.BlockSpec((1, 2, tn), lambda l, j: (l, 0, j)),
        out_shape=jax.ShapeDtypeStruct((DEPTH, 2, n), jnp.float32),
        compiler_params=pltpu.CompilerParams(dimension_semantics=("arbitrary", "arbitrary")),
        name="adaln_mod",
    )(c2, w_ada, b_ada.reshape(DEPTH, 1, n))


def _prenorm_kernel(h_ref, mod_ref, g_ref, o_ref):
    i = pl.program_id(0)
    x = h_ref[...]
    rows = i * TP + lax.broadcasted_iota(jnp.int32, (TP, 1), 0)
    is_ctx = rows < CTX
    shift = jnp.where(is_ctx, mod_ref[1:2, 0:D], mod_ref[0:1, 0:D])
    scale = jnp.where(is_ctx, mod_ref[1:2, D:2 * D], mod_ref[0:1, D:2 * D])
    y = x * lax.rsqrt(jnp.mean(x * x, axis=-1, keepdims=True) + EPS)
    y = y * g_ref[...]
    o_ref[...] = (y * (1.0 + scale) + shift).astype(o_ref.dtype)


def _prenorm_call(h, mod_l, g):
    t = h.shape[0]
    return pl.pallas_call(
        _prenorm_kernel,
        grid=(t // TP,),
        in_specs=[
            pl.BlockSpec((TP, D), lambda i: (i, 0)),
            pl.BlockSpec((2, 3 * D), lambda i: (0, 0)),
            pl.BlockSpec((1, D), lambda i: (0, 0)),
        ],
        out_specs=pl.BlockSpec((TP, D), lambda i: (i, 0)),
        out_shape=jax.ShapeDtypeStruct((t, D), jnp.bfloat16),
        compiler_params=pltpu.CompilerParams(dimension_semantics=("arbitrary",)),
        name="prenorm",
    )(h, mod_l, g)


J_K0, J_K1 = 2, 4
NJ = 2 + ZW // TN


def _inproj_kernel(x_ref, w_ref, wkt_ref, gq_ref, gk_ref, gv_ref, cq_ref, sq_ref, ck_ref, sk_ref,
                   gmat_ref, z_ref, kt_ref, acc_ref):
    j = pl.program_id(1)
    is_k = (j >= J_K0) & (j < J_K1)

    @pl.when(jnp.logical_not(is_k))
    def _():
        acc_ref[...] = jnp.dot(x_ref[...], w_ref[...], preferred_element_type=jnp.float32)

    @pl.when(j < J_K0)
    def _():
        for g in range(TN // V7X_LANES):
            z = acc_ref[:, g * V7X_LANES:(g + 1) * V7X_LANES]
            ss = jnp.dot((z * z).astype(jnp.bfloat16), gmat_ref[...],
                         preferred_element_type=jnp.float32)
            qn = z * lax.rsqrt(ss * (1.0 / QK) + EPS) * gq_ref[...]
            lane = lax.broadcasted_iota(jnp.int32, qn.shape, 1)
            partner = jnp.where((lane % (2 * ROPE_F)) < ROPE_F,
                                pltpu.roll(qn, V7X_LANES - ROPE_F, 1),
                                pltpu.roll(qn, ROPE_F, 1))
            qr = qn * cq_ref[...] + partner * sq_ref[...]
            z_ref[:, g * V7X_LANES:(g + 1) * V7X_LANES] = (
                qr * (QK ** -0.5 * LOG2E)).astype(z_ref.dtype)

    @pl.when(is_k)
    def _():
        zt = lax.dot_general(wkt_ref[...], x_ref[...], (((1,), (1,)), ((), ())),
                             preferred_element_type=jnp.float32)
        gk = gk_ref[...]
        for hm in range(TN // QK):
            blk = zt[hm * QK:(hm + 1) * QK]
            inv = lax.rsqrt(jnp.sum(blk * blk, axis=0, keepdims=True) * (1.0 / QK) + EPS)
            kn = blk * inv * gk
            for ax in range(2):
                x1 = kn[ax * 2 * ROPE_F:ax * 2 * ROPE_F + ROPE_F]
                x2 = kn[ax * 2 * ROPE_F + ROPE_F:(ax + 1) * 2 * ROPE_F]
                cs = ck_ref[ax * ROPE_F:(ax + 1) * ROPE_F]
                sn = sk_ref[ax * ROPE_F:(ax + 1) * ROPE_F]
                base = hm * QK + ax * 2 * ROPE_F
                kt_ref[0, base:base + ROPE_F, :] = (x1 * cs - x2 * sn).astype(kt_ref.dtype)
                kt_ref[0, base + ROPE_F:base + 2 * ROPE_F, :] = (
                    x2 * cs + x1 * sn).astype(kt_ref.dtype)

    def seg(lo, hi):
        return (j >= J_K1 - 2 + lo // TN) & (j < J_K1 - 2 + hi // TN)

    @pl.when(seg(ZC_V, ZC_GA))
    def _():
        z_ref[...] = acc_ref[...].astype(z_ref.dtype)

    @pl.when(seg(ZC_GA, ZC_U) | seg(ZC_GB, ZC_MG))
    def _():
        z = acc_ref[...]
        z_ref[...] = (z * _sigmoid(z)).astype(z_ref.dtype)

    @pl.when(seg(ZC_U, ZC_VN))
    def _():
        z_ref[...] = _gelu_tanh(acc_ref[...]).astype(z_ref.dtype)

    @pl.when(seg(ZC_VN, ZC_GB))
    def _():
        for g in range(TN // GDIM):
            v = _gelu_tanh(acc_ref[:, g * GDIM:(g + 1) * GDIM])
            vn = v * lax.rsqrt(jnp.mean(v * v, axis=-1, keepdims=True) + EPS)
            z_ref[:, g * GDIM:(g + 1) * GDIM] = (
                vn * gv_ref[:, g * GDIM:(g + 1) * GDIM]).astype(z_ref.dtype)

    @pl.when(seg(ZC_MG, ZW))
    def _():
        z_ref[...] = _sigmoid(acc_ref[...]).astype(z_ref.dtype)


def _inproj_call(hn, w, wkt, gq, gk, gv, cq, sq, ck, sk, gmat):
    t = hn.shape[0]

    def wtile(i, j):
        return jnp.where(j < J_K0, j, jnp.maximum(j - 2, 1))

    def ktile(i, j):
        return jnp.clip(j - J_K0, 0, 1)

    def gvtile(i, j):
        return jnp.clip(j - (J_K1 - 2 + ZC_VN // TN), 0, 1)

    return pl.pallas_call(
        _inproj_kernel,
        grid=(t // TM, NJ),
        in_specs=[
            pl.BlockSpec((TM, D), lambda i, j: (i, 0)),
            pl.BlockSpec((D, TN), lambda i, j: (0, wtile(i, j))),
            pl.BlockSpec((TN, D), lambda i, j: (ktile(i, j), 0)),
            pl.BlockSpec((1, V7X_LANES), lambda i, j: (0, 0)),
            pl.BlockSpec((QK, 1), lambda i, j: (0, 0)),
            pl.BlockSpec((1, TN), lambda i, j: (0, gvtile(i, j))),
            pl.BlockSpec((TM, V7X_LANES), lambda i, j: (i, 0)),
            pl.BlockSpec((TM, V7X_LANES), lambda i, j: (i, 0)),
            pl.BlockSpec((2 * ROPE_F, TM), lambda i, j: (0, i)),
            pl.BlockSpec((2 * ROPE_F, TM), lambda i, j: (0, i)),
            pl.BlockSpec((V7X_LANES, V7X_LANES), lambda i, j: (0, 0)),
        ],
        out_specs=[
            pl.BlockSpec((TM, TN), lambda i, j: (i, wtile(i, j))),
            pl.BlockSpec((1, TN, TM), lambda i, j: (i, ktile(i, j), 0)),
        ],
        out_shape=[
            jax.ShapeDtypeStruct((t, ZW), jnp.bfloat16),
            jax.ShapeDtypeStruct((t // TM, 2 * HEADS * QK, TM), jnp.bfloat16),
        ],
        scratch_shapes=[pltpu.VMEM((TM, TN), jnp.float32)],
        compiler_params=pltpu.CompilerParams(
            dimension_semantics=("arbitrary", "arbitrary"),
            vmem_limit_bytes=56 * 1024 * 1024),
        name="inproj",
    )(hn, w, wkt, gq, gk, gv, cq, sq, ck, sk, gmat)


def _attn_kernel(lam_ref, q_ref, kt_ref, v_ref, ga_ref, g_ref, o_ref, *, lam_init, nchunks):
    i = pl.program_id(1)
    q = q_ref[...]
    lane = lax.broadcasted_iota(jnp.int32, q.shape, 1)
    zero = jnp.zeros_like(q)
    qs = (jnp.where(lane < QK, q, zero), jnp.where(lane >= QK, q, zero))

    lv = lam_ref[...]
    lam = (jnp.exp(jnp.sum(lv[0:1] * lv[1:2], axis=-1, keepdims=True))
           - jnp.exp(jnp.sum(lv[2:3] * lv[3:4], axis=-1, keepdims=True)) + lam_init)

    def finish(o1, o2):
        d = o1 - lam * o2
        dn = d * lax.rsqrt(jnp.mean(d * d, axis=-1, keepdims=True) + EPS)
        dn = dn * g_ref[...] * (1.0 - lam_init)
        o_ref[...] = (dn * ga_ref[...].astype(jnp.float32)).astype(o_ref.dtype)

    @pl.when(i == 0)
    def _():
        kt = kt_ref[0, :, 0:CTX]
        v = v_ref[0:CTX, :]
        outs = []
        for qm in qs:
            s = jnp.dot(qm, kt, preferred_element_type=jnp.float32)
            p = jnp.exp2(s - jnp.max(s, axis=-1, keepdims=True))
            l = jnp.sum(p, axis=-1, keepdims=True)
            o = jnp.dot(p.astype(v.dtype), v, preferred_element_type=jnp.float32)
            outs.append(o / l)
        finish(outs[0], outs[1])

    @pl.when(i > 0)
    def _():
        def body(c, carry):
            kt = kt_ref[c]
            v = v_ref[pl.ds(pl.multiple_of(c * TM, TM), TM), :]
            new = []
            for qm, (m, l, acc) in zip(qs, carry):
                s = jnp.dot(qm, kt, preferred_element_type=jnp.float32)
                m_new = jnp.maximum(m, jnp.max(s, axis=-1, keepdims=True))
                alpha = jnp.exp2(m - m_new)
                p = jnp.exp2(s - m_new)
                l_new = alpha * l + jnp.sum(p, axis=-1, keepdims=True)
                acc_new = alpha * acc + jnp.dot(p.astype(v.dtype), v,
                                                preferred_element_type=jnp.float32)
                new.append((m_new, l_new, acc_new))
            return tuple(new)

        init = tuple((jnp.full((TQ, 1), -jnp.inf, jnp.float32),
                      jnp.zeros((TQ, 1), jnp.float32),
                      jnp.zeros((TQ, DV), jnp.float32)) for _ in range(2))
        (m1, l1, a1), (m2, l2, a2) = lax.fori_loop(0, nchunks, body, init)
        finish(a1 / l1, a2 / l2)


def _attn_call(lamv, z, kt3, g_sub, lam_init):
    t = z.shape[0]
    nchunks = kt3.shape[0]
    kern = functools.partial(_attn_kernel, lam_init=lam_init, nchunks=nchunks)
    return pl.pallas_call(
        kern,
        grid=(HEADS, t // TQ),
        in_specs=[
            pl.BlockSpec((4, QK), lambda h, i: (0, 0)),
            pl.BlockSpec((TQ, DV), lambda h, i: (i, ZC_Q // DV + h)),
            pl.BlockSpec((nchunks, 2 * QK, TM), lambda h, i: (0, h, 0)),
            pl.BlockSpec((t, DV), lambda h, i: (0, ZC_V // DV + h)),
            pl.BlockSpec((TQ, DV), lambda h, i: (i, ZC_GA // DV + h)),
            pl.BlockSpec((1, DV), lambda h, i: (0, 0)),
        ],
        out_specs=pl.BlockSpec((TQ, DV), lambda h, i: (i, h)),
        out_shape=jax.ShapeDtypeStruct((t, AW), jnp.bfloat16),
        compiler_params=pltpu.CompilerParams(
            dimension_semantics=("arbitrary", "arbitrary"),
            vmem_limit_bytes=48 * 1024 * 1024),
        name="diff_attn",
    )(lamv, z, kt3, z, z, g_sub)


def _merge_kernel(h_ref, mod_ref, ag_ref, u_ref, vn_ref, gb_ref, ma_ref, mb_ref, ws_ref, bs_ref,
                  wpa_ref, wpb_ref, wo_ref, o_ref, bo_ref):
    i = pl.program_id(0)
    for ck in range(TB // CHUNK):
        r0 = ck * CHUNK
        for g in range(GROUPS):
            c0 = g * GDIM
            s = jnp.dot(ws_ref[g], vn_ref[r0:r0 + CHUNK, c0:c0 + GDIM],
                        preferred_element_type=jnp.float32) + bs_ref[g]
            b = (u_ref[r0:r0 + CHUNK, c0:c0 + GDIM].astype(jnp.float32) * s
                 * gb_ref[r0:r0 + CHUNK, c0:c0 + GDIM].astype(jnp.float32))
            bo_ref[r0:r0 + CHUNK, c0:c0 + GDIM] = b.astype(bo_ref.dtype)
    ya = jnp.dot(ag_ref[...], wpa_ref[...], preferred_element_type=jnp.float32)
    yb = jnp.dot(bo_ref[...], wpb_ref[...], preferred_element_type=jnp.float32)
    mix = ma_ref[...].astype(jnp.float32) * ya + mb_ref[...].astype(jnp.float32) * yb
    out = jnp.dot(mix.astype(jnp.bfloat16), wo_ref[...], preferred_element_type=jnp.float32)
    gate = jnp.where(i == 0, mod_ref[1:2, 2 * D:3 * D], mod_ref[0:1, 2 * D:3 * D])
    o_ref[...] = h_ref[...] + gate * out


def _merge_call(h, mod_l, ag, z, ws, bs, wpa, wpb, wo):
    t = h.shape[0]
    const = dict(pipeline_mode=pl.Buffered(1))
    return pl.pallas_call(
        _merge_kernel,
        grid=(t // TB,),
        in_specs=[
            pl.BlockSpec((TB, D), lambda i: (i, 0)),
            pl.BlockSpec((2, 3 * D), lambda i: (0, 0)),
            pl.BlockSpec((TB, AW), lambda i: (i, 0)),
            pl.BlockSpec((TB, BW), lambda i: (i, ZC_U // BW)),
            pl.BlockSpec((TB, BW), lambda i: (i, ZC_VN // BW)),
            pl.BlockSpec((TB, BW), lambda i: (i, ZC_GB // BW)),
            pl.BlockSpec((TB, D), lambda i: (i, ZC_MG // D)),
            pl.BlockSpec((TB, D), lambda i: (i, ZC_MG // D + 1)),
            pl.BlockSpec((GROUPS, CHUNK, CHUNK), lambda i: (0, 0, 0)),
            pl.BlockSpec((GROUPS, CHUNK, GDIM), lambda i: (0, 0, 0)),
            pl.BlockSpec((AW, D), lambda i: (0, 0), **const),
            pl.BlockSpec((BW, D), lambda i: (0, 0), **const),
            pl.BlockSpec((D, D), lambda i: (0, 0), **const),
        ],
        out_specs=pl.BlockSpec((TB, D), lambda i: (i, 0)),
        out_shape=jax.ShapeDtypeStruct((t, D), jnp.float32),
        scratch_shapes=[pltpu.VMEM((TB, BW), jnp.bfloat16)],
        input_output_aliases={0: 0},
        compiler_params=pltpu.CompilerParams(
            dimension_semantics=("arbitrary",),
            vmem_limit_bytes=48 * 1024 * 1024),
        name="gmlp_merge",
    )(h, mod_l, ag, z, z, z, z, z, ws, bs, wpa, wpb, wo)


def _rope_tables(seq):
    rows = seq // GRID_W
    row = jnp.repeat(jnp.arange(rows, dtype=jnp.int32), GRID_W).astype(jnp.float32)
    col = jnp.tile(jnp.arange(GRID_W, dtype=jnp.int32), rows).astype(jnp.float32)
    inv = ROPE_BASE ** (-jnp.arange(ROPE_F, dtype=jnp.float32) / ROPE_F)
    ang = jnp.concatenate([row[:, None] * inv, col[:, None] * inv], axis=1)
    cos = jnp.concatenate([jnp.ones((CTX, 2 * ROPE_F), jnp.float32), jnp.cos(ang)], axis=0)
    sin = jnp.concatenate([jnp.zeros((CTX, 2 * ROPE_F), jnp.float32), jnp.sin(ang)], axis=0)
    cr, cc = cos[:, :ROPE_F], cos[:, ROPE_F:]
    sr, sc = sin[:, :ROPE_F], sin[:, ROPE_F:]
    cq = jnp.tile(jnp.concatenate([cr, cr, cc, cc], axis=1), (1, 2))
    sq = jnp.tile(jnp.concatenate([-sr, sr, -sc, sc], axis=1), (1, 2))
    return cq, sq, cos.T, sin.T


def kernel(x, c, ctx, c_ctx, w_ada, b_ada, norm_g, w_in, q_norm_g, k_norm_g, lam_q1, lam_k1,
           lam_q2, lam_k2, subln_g, v_norm_g, w_spatial, b_spatial, w_proj_a, w_proj_b, w_out):
    assert x.shape == (1, GRID_W * GRID_W * 4, D) and ctx.shape == (1, CTX, D)
    seq = x.shape[1]
    t = CTX + seq
    assert t % TM == 0 and t % TQ == 0 and t % TP == 0 and TB == CTX

    bf = jnp.bfloat16
    cq, sq, ck, sk = _rope_tables(seq)
    mod = _mod_call(jnp.stack([c[0], c_ctx], axis=1), w_ada, b_ada)
    gmat = (jnp.arange(V7X_LANES)[:, None] // QK == jnp.arange(V7X_LANES)[None, :] // QK).astype(bf)

    w_main = jnp.concatenate([w_in[:, :, :HEADS * 2 * QK], w_in[:, :, 2 * HEADS * 2 * QK:]],
                             axis=-1).astype(bf)
    w_kt = jnp.swapaxes(w_in[:, :, HEADS * 2 * QK:2 * HEADS * 2 * QK], 1, 2).astype(bf)
    ws = w_spatial.astype(bf)
    bs = jnp.broadcast_to(b_spatial[..., None], b_spatial.shape + (GDIM,))
    wpa, wpb, wo = w_proj_a.astype(bf), w_proj_b.astype(bf), w_out.astype(bf)
    gq = jnp.tile(q_norm_g, (1, V7X_LANES // QK))
    lamv = jnp.stack([lam_q1, lam_k1, lam_q2, lam_k2], axis=1)

    h = jnp.concatenate([ctx[0], x[0]], axis=0)
    for l in range(DEPTH):
        lam_init = 0.8 - 0.6 * math.exp(-0.3 * l)
        hn = _prenorm_call(h, mod[l], norm_g[l][None, :])
        z, kt3 = _inproj_call(hn, w_main[l], w_kt[l], gq[l][None, :], k_norm_g[l][:, None],
                              v_norm_g[l].reshape(1, BW), cq, sq, ck, sk, gmat)
        ag = _attn_call(lamv[l], z, kt3, subln_g[l][None, :], lam_init)
        h = _merge_call(h, mod[l], ag, z, ws[l], bs[l], wpa[l], wpb[l], wo[l])
    return h[CTX:][None]
```

```python
import functools
import math

import jax
import jax.numpy as jnp
from jax import lax
from jax.experimental import pallas as pl
from jax.experimental.pallas import tpu as pltpu

D = 2048
DEPTH = 4
GRID_W = 64
CTX = 256
HEADS = 8
QK = 64
DV = 128
AW = HEADS * DV
ROPE_F = 16
ROPE_BASE = 10000.0
GROUPS = 8
CHUNK = 128
GDIM = 128
BW = GROUPS * GDIM
EPS = 1e-6
LOG2E = 1.4426950408889634

V7X_LANES = 128
V7X_VMEM_BYTES = 64 * 1024 * 1024

ZC_Q, ZC_K, ZC_GA, ZC_U, ZC_VN, ZC_GB, ZC_MG = 0, 1024, 2048, 3072, 4096, 5120, 6144
ZW = 6144 + 2 * D

SHIFT_SAFE = 50.0
BOUND_SLACK = 1.0 + 2.0 ** -7

TM = 1280
TN = 512
TQ = 256
TB = 256
TP = 640


def _sigmoid(x):
    return 1.0 / (1.0 + jnp.exp(-x))


def _gelu_tanh(x):
    return 0.5 * x * (1.0 + jnp.tanh(math.sqrt(2.0 / math.pi) * (x + 0.044715 * (x * x * x))))


def _mod_kernel(c_ref, w_ref, b_ref, o_ref):
    c = c_ref[...]
    s = c * _sigmoid(c)
    w = w_ref[0]
    b = b_ref[0]
    r0 = jnp.sum(w * s[:, 0:1], axis=0, keepdims=True) + b
    r1 = jnp.sum(w * s[:, 1:2], axis=0, keepdims=True) + b
    o_ref[0] = jnp.concatenate([r0, r1], axis=0)


def _mod_call(c2, w_ada, b_ada):
    tn = 512
    n = w_ada.shape[-1]
    return pl.pallas_call(
        _mod_kernel,
        grid=(DEPTH, n // tn),
        in_specs=[
            pl.BlockSpec((D, 2), lambda l, j: (0, 0)),
            pl.BlockSpec((1, D, tn), lambda l, j: (l, 0, j)),
            pl.BlockSpec((1, 1, tn), lambda l, j: (l, 0, j)),
        ],
        out_specs=pl.BlockSpec((1, 2, tn), lambda l, j: (l, 0, j)),
        out_shape=jax.ShapeDtypeStruct((DEPTH, 2, n), jnp.float32),
        compiler_params=pltpu.CompilerParams(dimension_semantics=("arbitrary", "arbitrary")),
        name="adaln_mod",
    )(c2, w_ada, b_ada.reshape(DEPTH, 1, n))


def _prenorm_kernel(h_ref, mod_ref, g_ref, o_ref):
    i = pl.program_id(0)
    x = h_ref[...]
    rows = i * TP + lax.broadcasted_iota(jnp.int32, (TP, 1), 0)
    is_ctx = rows < CTX
    shift = jnp.where(is_ctx, mod_ref[1:2, 0:D], mod_ref[0:1, 0:D])
    scale = jnp.where(is_ctx, mod_ref[1:2, D:2 * D], mod_ref[0:1, D:2 * D])
    y = x * lax.rsqrt(jnp.mean(x * x, axis=-1, keepdims=True) + EPS)
    y = y * g_ref[...]
    o_ref[...] = (y * (1.0 + scale) + shift).astype(o_ref.dtype)


def _prenorm_call(h, mod_l, g):
    t = h.shape[0]
    return pl.pallas_call(
        _prenorm_kernel,
        grid=(t // TP,),
        in_specs=[
            pl.BlockSpec((TP, D), lambda i: (i, 0)),
            pl.BlockSpec((2, 3 * D), lambda i: (0, 0)),
            pl.BlockSpec((1, D), lambda i: (0, 0)),
        ],
        out_specs=pl.BlockSpec((TP, D), lambda i: (i, 0)),
        out_shape=jax.ShapeDtypeStruct((t, D), jnp.bfloat16),
        compiler_params=pltpu.CompilerParams(dimension_semantics=("arbitrary",)),
        name="prenorm",
    )(h, mod_l, g)


J_VT0, J_VT1 = 4, 6
NJ = 2 + ZW // TN


def _inproj_kernel(x_ref, w_ref, wvt_ref, gq_ref, gk_ref, gv_ref, cq_ref, sq_ref, gmat_ref,
                   z_ref, vt_ref, acc_ref):
    j = pl.program_id(1)
    is_vt = (j >= J_VT0) & (j < J_VT1)

    @pl.when(jnp.logical_not(is_vt))
    def _():
        acc_ref[...] = jnp.dot(x_ref[...], w_ref[...], preferred_element_type=jnp.float32)

    def qk_epilogue(gain_ref, out_scale):
        for g in range(TN // V7X_LANES):
            z = acc_ref[:, g * V7X_LANES:(g + 1) * V7X_LANES]
            ss = jnp.dot((z * z).astype(jnp.bfloat16), gmat_ref[...],
                         preferred_element_type=jnp.float32)
            qn = z * lax.rsqrt(ss * (1.0 / QK) + EPS) * gain_ref[...]
            lane = lax.broadcasted_iota(jnp.int32, qn.shape, 1)
            partner = jnp.where((lane % (2 * ROPE_F)) < ROPE_F,
                                pltpu.roll(qn, V7X_LANES - ROPE_F, 1),
                                pltpu.roll(qn, ROPE_F, 1))
            qr = qn * cq_ref[...] + partner * sq_ref[...]
            if out_scale != 1.0:
                qr = qr * out_scale
            z_ref[:, g * V7X_LANES:(g + 1) * V7X_LANES] = qr.astype(z_ref.dtype)

    @pl.when(j < ZC_K // TN)
    def _():
        qk_epilogue(gq_ref, QK ** -0.5 * LOG2E)

    @pl.when((j >= ZC_K // TN) & (j < J_VT0))
    def _():
        qk_epilogue(gk_ref, 1.0)

    @pl.when(is_vt)
    def _():
        vt_ref[0] = lax.dot_general(wvt_ref[...], x_ref[...], (((1,), (1,)), ((), ())),
                                    preferred_element_type=jnp.float32).astype(vt_ref.dtype)

    def seg(lo, hi):
        return (j >= 2 + lo // TN) & (j < 2 + hi // TN)

    @pl.when(seg(ZC_GA, ZC_U) | seg(ZC_GB, ZC_MG))
    def _():
        z = acc_ref[...]
        z_ref[...] = (z * _sigmoid(z)).astype(z_ref.dtype)

    @pl.when(seg(ZC_U, ZC_VN))
    def _():
        z_ref[...] = _gelu_tanh(acc_ref[...]).astype(z_ref.dtype)

    @pl.when(seg(ZC_VN, ZC_GB))
    def _():
        for g in range(TN // GDIM):
            v = _gelu_tanh(acc_ref[:, g * GDIM:(g + 1) * GDIM])
            vn = v * lax.rsqrt(jnp.mean(v * v, axis=-1, keepdims=True) + EPS)
            z_ref[:, g * GDIM:(g + 1) * GDIM] = (
                vn * gv_ref[:, g * GDIM:(g + 1) * GDIM]).astype(z_ref.dtype)

    @pl.when(seg(ZC_MG, ZW))
    def _():
        z_ref[...] = _sigmoid(acc_ref[...]).astype(z_ref.dtype)


def _inproj_call(hn, w, wvt, gq, gk, gv, cq, sq, gmat):
    t = hn.shape[0]

    def wtile(i, j):
        return jnp.where(j < J_VT0, j, jnp.maximum(j - 2, J_VT0 - 1))

    def vtile(i, j):
        return jnp.clip(j - J_VT0, 0, 1)

    def gvtile(i, j):
        return jnp.clip(j - (2 + ZC_VN // TN), 0, 1)

    return pl.pallas_call(
        _inproj_kernel,
        grid=(t // TM, NJ),
        in_specs=[
            pl.BlockSpec((TM, D), lambda i, j: (i, 0)),
            pl.BlockSpec((D, TN), lambda i, j: (0, wtile(i, j))),
            pl.BlockSpec((TN, D), lambda i, j: (vtile(i, j), 0)),
            pl.BlockSpec((1, V7X_LANES), lambda i, j: (0, 0)),
            pl.BlockSpec((1, V7X_LANES), lambda i, j: (0, 0)),
            pl.BlockSpec((1, TN), lambda i, j: (0, gvtile(i, j))),
            pl.BlockSpec((TM, V7X_LANES), lambda i, j: (i, 0)),
            pl.BlockSpec((TM, V7X_LANES), lambda i, j: (i, 0)),
            pl.BlockSpec((V7X_LANES, V7X_LANES), lambda i, j: (0, 0)),
        ],
        out_specs=[
            pl.BlockSpec((TM, TN), lambda i, j: (i, wtile(i, j))),
            pl.BlockSpec((1, TN, TM), lambda i, j: (i, vtile(i, j), 0)),
        ],
        out_shape=[
            jax.ShapeDtypeStruct((t, ZW), jnp.bfloat16),
            jax.ShapeDtypeStruct((t // TM, AW, TM), jnp.bfloat16),
        ],
        scratch_shapes=[pltpu.VMEM((TM, TN), jnp.float32)],
        compiler_params=pltpu.CompilerParams(
            dimension_semantics=("arbitrary", "arbitrary"),
            vmem_limit_bytes=56 * 1024 * 1024),
        name="inproj",
    )(hn, w, wvt, gq, gk, gv, cq, sq, gmat)


def _attn_kernel(lam_ref, q_ref, k_ref, vt_ref, ga_ref, g_ref, gmat_ref, o_ref, kmax_ref,
                 *, lam_init, nchunks):
    i = pl.program_id(1)
    f32, bf = jnp.float32, jnp.bfloat16

    @pl.when(i == 0)
    def _():
        def kbody(c, mx):
            kk = k_ref[pl.ds(pl.multiple_of(c * TM, TM), TM), :].astype(f32)
            nrm = jnp.dot((kk * kk).astype(bf), gmat_ref[...], preferred_element_type=f32)
            return jnp.maximum(mx, jnp.max(nrm.reshape(TM // 8, 8, 2 * QK), axis=0))
        mx = lax.fori_loop(0, nchunks, kbody, jnp.zeros((8, 2 * QK), f32))
        kmax_ref[...] = jnp.sqrt(jnp.max(mx, axis=0, keepdims=True)) * BOUND_SLACK

    qf = q_ref[...].astype(f32)
    lane = lax.broadcasted_iota(jnp.int32, qf.shape, 1)
    in1 = lane < QK
    qsq = qf * qf
    kmax = kmax_ref[...]
    shifts = (jnp.sqrt(jnp.sum(jnp.where(in1, qsq, 0.0), axis=-1, keepdims=True)) * kmax[:, 0:1],
              jnp.sqrt(jnp.sum(jnp.where(in1, 0.0, qsq), axis=-1, keepdims=True))
              * kmax[:, QK:QK + 1])
    qms = (jnp.where(in1, qf, 0.0), jnp.where(in1, 0.0, qf))
    qaug = tuple(jnp.concatenate([qm, jnp.where(lane == 0, -sh, 0.0)], axis=1).T.astype(bf)
                 for qm, sh in zip(qms, shifts))
    shift_max = jnp.maximum(jnp.max(shifts[0]), jnp.max(shifts[1]))

    lv = lam_ref[...]
    lam = (jnp.exp(jnp.sum(lv[0:1] * lv[1:2], axis=-1, keepdims=True))
           - jnp.exp(jnp.sum(lv[2:3] * lv[3:4], axis=-1, keepdims=True)) + lam_init)

    def finish(o1, o2):
        d = o1 - lam * o2
        dn = d * lax.rsqrt(jnp.mean(d * d, axis=0, keepdims=True) + EPS)
        dn = dn * g_ref[...] * (1.0 - lam_init)
        o_ref[...] = (dn.T * ga_ref[...].astype(f32)).astype(o_ref.dtype)

    def online_step(k, vt, qt, m, l, acc):
        st = jnp.dot(k, qt, preferred_element_type=f32)
        m_new = jnp.maximum(m, jnp.max(st, axis=0, keepdims=True))
        alpha = jnp.exp2(m - m_new)
        p = jnp.exp2(st - m_new)
        l_new = alpha * l + jnp.sum(p, axis=0, keepdims=True)
        acc_new = alpha * acc + jnp.dot(vt, p.astype(bf), preferred_element_type=f32)
        return m_new, l_new, acc_new

    def online_init():
        return (jnp.full((1, TQ), -jnp.inf, f32), jnp.zeros((1, TQ), f32),
                jnp.zeros((DV, TQ), f32))

    @pl.when(i == 0)
    def _():
        k = k_ref[0:CTX, :]
        vt = vt_ref[0, :, 0:CTX]
        outs = []
        for qa in qaug:
            _, l, acc = online_step(k, vt, qa[0:2 * QK], *online_init())
            outs.append(acc / l)
        finish(outs[0], outs[1])

    @pl.when((i > 0) & (shift_max <= SHIFT_SAFE))
    def _():
        ones_blk = jnp.ones((TM, V7X_LANES), bf)

        def body(c, carry):
            k = k_ref[pl.ds(pl.multiple_of(c * TM, TM), TM), :]
            kaug = jnp.concatenate([k, ones_blk], axis=1)
            vt = vt_ref[c]
            sts = [jnp.dot(kaug, qa, preferred_element_type=f32) for qa in qaug]
            ps = [jnp.exp2(st) for st in sts]
            new = []
            for p, (ls, acc) in zip(ps, carry):
                ls = ls + jnp.sum(p.reshape(TM // 8, 8, TQ), axis=0)
                acc = acc + jnp.dot(vt, p.astype(bf), preferred_element_type=f32)
                new.append((ls, acc))
            return tuple(new)

        init = tuple((jnp.zeros((8, TQ), f32), jnp.zeros((DV, TQ), f32)) for _ in range(2))
        (l1, a1), (l2, a2) = lax.fori_loop(0, nchunks, body, init)
        finish(a1 / jnp.sum(l1, axis=0, keepdims=True), a2 / jnp.sum(l2, axis=0, keepdims=True))

    @pl.when((i > 0) & jnp.logical_not(shift_max <= SHIFT_SAFE))
    def _():
        def body(c, carry):
            k = k_ref[pl.ds(pl.multiple_of(c * TM, TM), TM), :]
            vt = vt_ref[c]
            return tuple(online_step(k, vt, qa[0:2 * QK], *st) for qa, st in zip(qaug, carry))

        (_, l1, a1), (_, l2, a2) = lax.fori_loop(0, nchunks, body, (online_init(), online_init()))
        finish(a1 / l1, a2 / l2)


def _attn_call(lamv, z, vt3, g_sub, gmat, lam_init):
    t = z.shape[0]
    nchunks = vt3.shape[0]
    kern = functools.partial(_attn_kernel, lam_init=lam_init, nchunks=nchunks)
    return pl.pallas_call(
        kern,
        grid=(HEADS, t // TQ),
        in_specs=[
            pl.BlockSpec((4, QK), lambda h, i: (0, 0)),
            pl.BlockSpec((TQ, 2 * QK), lambda h, i: (i, ZC_Q // DV + h)),
            pl.BlockSpec((t, 2 * QK), lambda h, i: (0, ZC_K // DV + h)),
            pl.BlockSpec((nchunks, DV, TM), lambda h, i: (0, h, 0)),
            pl.BlockSpec((TQ, DV), lambda h, i: (i, ZC_GA // DV + h)),
            pl.BlockSpec((DV, 1), lambda h, i: (0, 0)),
            pl.BlockSpec((V7X_LANES, V7X_LANES), lambda h, i: (0, 0)),
        ],
        out_specs=pl.BlockSpec((TQ, DV), lambda h, i: (i, h)),
        out_shape=jax.ShapeDtypeStruct((t, AW), jnp.bfloat16),
        scratch_shapes=[pltpu.VMEM((1, 2 * QK), jnp.float32)],
        compiler_params=pltpu.CompilerParams(
            dimension_semantics=("arbitrary", "arbitrary"),
            vmem_limit_bytes=48 * 1024 * 1024),
        name="diff_attn",
    )(lamv, z, z, vt3, z, g_sub, gmat)


def _merge_kernel(h_ref, mod_ref, ag_ref, u_ref, vn_ref, gb_ref, ma_ref, mb_ref, ws_ref, bs_ref,
                  wpa_ref, wpb_ref, wo_ref, o_ref, bo_ref):
    i = pl.program_id(0)
    for ck in range(TB // CHUNK):
        r0 = ck * CHUNK
        for g in range(GROUPS):
            c0 = g * GDIM
            s = jnp.dot(ws_ref[g], vn_ref[r0:r0 + CHUNK, c0:c0 + GDIM],
                        preferred_element_type=jnp.float32) + bs_ref[g]
            b = (u_ref[r0:r0 + CHUNK, c0:c0 + GDIM].astype(jnp.float32) * s
                 * gb_ref[r0:r0 + CHUNK, c0:c0 + GDIM].astype(jnp.float32))
            bo_ref[r0:r0 + CHUNK, c0:c0 + GDIM] = b.astype(bo_ref.dtype)
    ya = jnp.dot(ag_ref[...], wpa_ref[...], preferred_element_type=jnp.float32)
    yb = jnp.dot(bo_ref[...], wpb_ref[...], preferred_element_type=jnp.float32)
    mix = ma_ref[...].astype(jnp.float32) * ya + mb_ref[...].astype(jnp.float32) * yb
    out = jnp.dot(mix.astype(jnp.bfloat16), wo_ref[...], preferred_element_type=jnp.float32)
    gate = jnp.where(i == 0, mod_ref[1:2, 2 * D:3 * D], mod_ref[0:1, 2 * D:3 * D])
    o_ref[...] = h_ref[...] + gate * out


def _merge_call(h, mod_l, ag, z, ws, bs, wpa, wpb, wo):
    t = h.shape[0]
    const = dict(pipeline_mode=pl.Buffered(1))
    return pl.pallas_call(
        _merge_kernel,
        grid=(t // TB,),
        in_specs=[
            pl.BlockSpec((TB, D), lambda i: (i, 0)),
            pl.BlockSpec((2, 3 * D), lambda i: (0, 0)),
            pl.BlockSpec((TB, AW), lambda i: (i, 0)),
            pl.BlockSpec((TB, BW), lambda i: (i, ZC_U // BW)),
            pl.BlockSpec((TB, BW), lambda i: (i, ZC_VN // BW)),
            pl.BlockSpec((TB, BW), lambda i: (i, ZC_GB // BW)),
            pl.BlockSpec((TB, D), lambda i: (i, ZC_MG // D)),
            pl.BlockSpec((TB, D), lambda i: (i, ZC_MG // D + 1)),
            pl.BlockSpec((GROUPS, CHUNK, CHUNK), lambda i: (0, 0, 0)),
            pl.BlockSpec((GROUPS, CHUNK, GDIM), lambda i: (0, 0, 0)),
            pl.BlockSpec((AW, D), lambda i: (0, 0), **const),
            pl.BlockSpec((BW, D), lambda i: (0, 0), **const),
            pl.BlockSpec((D, D), lambda i: (0, 0), **const),
        ],
        out_specs=pl.BlockSpec((TB, D), lambda i: (i, 0)),
        out_shape=jax.ShapeDtypeStruct((t, D), jnp.float32),
        scratch_shapes=[pltpu.VMEM((TB, BW), jnp.bfloat16)],
        input_output_aliases={0: 0},
        compiler_params=pltpu.CompilerParams(
            dimension_semantics=("arbitrary",),
            vmem_limit_bytes=48 * 1024 * 1024),
        name="gmlp_merge",
    )(h, mod_l, ag, z, z, z, z, z, ws, bs, wpa, wpb, wo)


def _rope_tables(seq):
    rows = seq // GRID_W
    row = jnp.repeat(jnp.arange(rows, dtype=jnp.int32), GRID_W).astype(jnp.float32)
    col = jnp.tile(jnp.arange(GRID_W, dtype=jnp.int32), rows).astype(jnp.float32)
    inv = ROPE_BASE ** (-jnp.arange(ROPE_F, dtype=jnp.float32) / ROPE_F)
    ang = jnp.concatenate([row[:, None] * inv, col[:, None] * inv], axis=1)
    cos = jnp.concatenate([jnp.ones((CTX, 2 * ROPE_F), jnp.float32), jnp.cos(ang)], axis=0)
    sin = jnp.concatenate([jnp.zeros((CTX, 2 * ROPE_F), jnp.float32), jnp.sin(ang)], axis=0)
    cr, cc = cos[:, :ROPE_F], cos[:, ROPE_F:]
    sr, sc = sin[:, :ROPE_F], sin[:, ROPE_F:]
    cq = jnp.tile(jnp.concatenate([cr, cr, cc, cc], axis=1), (1, 2))
    sq = jnp.tile(jnp.concatenate([-sr, sr, -sc, sc], axis=1), (1, 2))
    return cq, sq


def kernel(x, c, ctx, c_ctx, w_ada, b_ada, norm_g, w_in, q_norm_g, k_norm_g, lam_q1, lam_k1,
           lam_q2, lam_k2, subln_g, v_norm_g, w_spatial, b_spatial, w_proj_a, w_proj_b, w_out):
    assert x.shape == (1, GRID_W * GRID_W * 4, D) and ctx.shape == (1, CTX, D)
    seq = x.shape[1]
    t = CTX + seq
    assert t % TM == 0 and t % TQ == 0 and t % TP == 0 and TB == CTX

    bf = jnp.bfloat16
    cq, sq = _rope_tables(seq)
    mod = _mod_call(jnp.stack([c[0], c_ctx], axis=1), w_ada, b_ada)
    gmat = (jnp.arange(V7X_LANES)[:, None] // QK == jnp.arange(V7X_LANES)[None, :] // QK).astype(bf)

    nqk = 2 * HEADS * 2 * QK
    w_main = jnp.concatenate([w_in[:, :, :nqk], w_in[:, :, nqk + AW:]], axis=-1).astype(bf)
    w_vt = jnp.swapaxes(w_in[:, :, nqk:nqk + AW], 1, 2).astype(bf)
    ws = w_spatial.astype(bf)
    bs = jnp.broadcast_to(b_spatial[..., None], b_spatial.shape + (GDIM,))
    wpa, wpb, wo = w_proj_a.astype(bf), w_proj_b.astype(bf), w_out.astype(bf)
    gq = jnp.tile(q_norm_g, (1, V7X_LANES // QK))
    gk = jnp.tile(k_norm_g, (1, V7X_LANES // QK))
    lamv = jnp.stack([lam_q1, lam_k1, lam_q2, lam_k2], axis=1)

    h = jnp.concatenate([ctx[0], x[0]], axis=0)
    for l in range(DEPTH):
        lam_init = 0.8 - 0.6 * math.exp(-0.3 * l)
        hn = _prenorm_call(h, mod[l], norm_g[l][None, :])
        z, vt3 = _inproj_call(hn, w_main[l], w_vt[l], gq[l][None, :], gk[l][None, :],
                              v_norm_g[l].reshape(1, BW), cq, sq, gmat)
        ag = _attn_call(lamv[l], z, vt3, subln_g[l][:, None], gmat, lam_init)
        h = _merge_call(h, mod[l], ag, z, ws[l], bs[l], wpa[l], wpb[l], wo[l])
    return h[CTX:][None]
```

```python
import functools
import math

import jax
import jax.numpy as jnp
from jax import lax
from jax.experimental import pallas as pl
from jax.experimental.pallas import tpu as pltpu

D = 2048
DEPTH = 4
GRID_W = 64
SEQ = 4 * GRID_W * GRID_W
CTX = 256
T = SEQ + CTX
HEADS = 8
QK = 64
DV = 128
AW = HEADS * DV
ROPE_F = 16
ROPE_BASE = 10000.0
GROUPS = 8
CHUNK = 128
GDIM = 128
BW = GROUPS * GDIM
EPS = 1e-6
LOG2E = 1.4426950408889634

V7X_LANES = 128
V7X_VMEM_BYTES = 64 * 1024 * 1024

ZC_Q, ZC_K, ZC_GA, ZC_U, ZC_VN, ZC_GB, ZC_MG = 0, 1024, 2048, 3072, 4096, 5120, 6144
ZW = 6144 + 2 * D

SHIFT_SAFE = 50.0
BOUND_SLACK = 1.0 + 2.0 ** -7

TM = 1280
TN = 512
TQ = 1024
TB = 256
TP = 640


def _sigmoid(x):
    return 1.0 / (1.0 + jnp.exp(-x))


def _gelu_tanh(x):
    return 0.5 * x * (1.0 + jnp.tanh(math.sqrt(2.0 / math.pi) * (x + 0.044715 * (x * x * x))))


def _mod_kernel(c_ref, w_ref, b_ref, o_ref):
    c = c_ref[...]
    s = c * _sigmoid(c)
    w = w_ref[0]
    b = b_ref[0]
    r0 = jnp.sum(w * s[:, 0:1], axis=0, keepdims=True) + b
    r1 = jnp.sum(w * s[:, 1:2], axis=0, keepdims=True) + b
    o_ref[0] = jnp.concatenate([r0, r1], axis=0)


def _mod_call(c2, w_ada, b_ada):
    tn = 512
    n = w_ada.shape[-1]
    return pl.pallas_call(
        _mod_kernel,
        grid=(DEPTH, n // tn),
        in_specs=[
            pl.BlockSpec((D, 2), lambda l, j: (0, 0)),
            pl.BlockSpec((1, D, tn), lambda l, j: (l, 0, j)),
            pl.BlockSpec((1, 1, tn), lambda l, j: (l, 0, j)),
        ],
        out_specs=pl.BlockSpec((1, 2, tn), lambda l, j: (l, 0, j)),
        out_shape=jax.ShapeDtypeStruct((DEPTH, 2, n), jnp.float32),
        compiler_params=pltpu.CompilerParams(dimension_semantics=("arbitrary", "arbitrary")),
        name="adaln_mod",
    )(c2, w_ada, b_ada.reshape(DEPTH, 1, n))


def _prenorm_kernel(h_ref, mod_ref, g_ref, o_ref):
    i = pl.program_id(0)
    x = h_ref[...]
    rows = i * TP + lax.broadcasted_iota(jnp.int32, (TP, 1), 0)
    is_ctx = rows >= SEQ
    shift = jnp.where(is_ctx, mod_ref[1:2, 0:D], mod_ref[0:1, 0:D])
    scale = jnp.where(is_ctx, mod_ref[1:2, D:2 * D], mod_ref[0:1, D:2 * D])
    y = x * lax.rsqrt(jnp.mean(x * x, axis=-1, keepdims=True) + EPS)
    y = y * g_ref[...]
    o_ref[...] = (y * (1.0 + scale) + shift).astype(o_ref.dtype)


def _prenorm_call(h, mod_l, g):
    t = h.shape[0]
    return pl.pallas_call(
        _prenorm_kernel,
        grid=(t // TP,),
        in_specs=[
            pl.BlockSpec((TP, D), lambda i: (i, 0)),
            pl.BlockSpec((2, 3 * D), lambda i: (0, 0)),
            pl.BlockSpec((1, D), lambda i: (0, 0)),
        ],
        out_specs=pl.BlockSpec((TP, D), lambda i: (i, 0)),
        out_shape=jax.ShapeDtypeStruct((t, D), jnp.bfloat16),
        compiler_params=pltpu.CompilerParams(dimension_semantics=("arbitrary",)),
        name="prenorm",
    )(h, mod_l, g)


J_VT0, J_VT1 = 4, 6
NJ = 2 + ZW // TN


def _inproj_kernel(x_ref, w_ref, wvt_ref, gq_ref, gk_ref, cq_ref, sq_ref, gmat_ref,
                   z_ref, vt_ref, acc_ref):
    j = pl.program_id(1)
    is_vt = (j >= J_VT0) & (j < J_VT1)

    def project():
        return jnp.dot(x_ref[...], w_ref[0].astype(jnp.bfloat16),
                       preferred_element_type=jnp.float32)

    @pl.when(j < J_VT0)
    def _():
        acc_ref[...] = project()

    @pl.when(j >= J_VT1)
    def _():
        z_ref[...] = project().astype(z_ref.dtype)

    def qk_epilogue(gain_ref, out_scale):
        for g in range(TN // V7X_LANES):
            z = acc_ref[:, g * V7X_LANES:(g + 1) * V7X_LANES]
            ss = jnp.dot((z * z).astype(jnp.bfloat16), gmat_ref[...],
                         preferred_element_type=jnp.float32)
            qn = z * lax.rsqrt(ss * (1.0 / QK) + EPS) * gain_ref[...]
            lane = lax.broadcasted_iota(jnp.int32, qn.shape, 1)
            partner = jnp.where((lane % (2 * ROPE_F)) < ROPE_F,
                                pltpu.roll(qn, V7X_LANES - ROPE_F, 1),
                                pltpu.roll(qn, ROPE_F, 1))
            qr = qn * cq_ref[...] + partner * sq_ref[...]
            if out_scale != 1.0:
                qr = qr * out_scale
            z_ref[:, g * V7X_LANES:(g + 1) * V7X_LANES] = qr.astype(z_ref.dtype)

    @pl.when(j < ZC_K // TN)
    def _():
        qk_epilogue(gq_ref, QK ** -0.5 * LOG2E)

    @pl.when((j >= ZC_K // TN) & (j < J_VT0))
    def _():
        qk_epilogue(gk_ref, 1.0)

    @pl.when(is_vt)
    def _():
        vt_ref[0] = lax.dot_general(wvt_ref[...], x_ref[...], (((1,), (1,)), ((), ())),
                                    preferred_element_type=jnp.float32).astype(vt_ref.dtype)


def _inproj_call(hn, w_in, layer, wvt, gq, gk, cq, sq, gmat):
    t = hn.shape[0]

    def wtile(i, j):
        return jnp.where((j >= J_VT0) & (j < J_VT1), J_VT0 - 1, j)

    def ztile(i, j):
        return jnp.where(j < J_VT0, j, jnp.maximum(j - 2, J_VT0 - 1))

    def vtile(i, j):
        return jnp.clip(j - J_VT0, 0, 1)

    return pl.pallas_call(
        _inproj_kernel,
        grid=(t // TM, NJ),
        in_specs=[
            pl.BlockSpec((TM, D), lambda i, j: (i, 0)),
            pl.BlockSpec((1, D, TN), lambda i, j: (layer, 0, wtile(i, j))),
            pl.BlockSpec((TN, D), lambda i, j: (vtile(i, j), 0)),
            pl.BlockSpec((1, V7X_LANES), lambda i, j: (0, 0)),
            pl.BlockSpec((1, V7X_LANES), lambda i, j: (0, 0)),
            pl.BlockSpec((TM, V7X_LANES), lambda i, j: (i, 0)),
            pl.BlockSpec((TM, V7X_LANES), lambda i, j: (i, 0)),
            pl.BlockSpec((V7X_LANES, V7X_LANES), lambda i, j: (0, 0)),
        ],
        out_specs=[
            pl.BlockSpec((TM, TN), lambda i, j: (i, ztile(i, j))),
            pl.BlockSpec((1, TN, TM), lambda i, j: (i, vtile(i, j), 0)),
        ],
        out_shape=[
            jax.ShapeDtypeStruct((t, ZW), jnp.bfloat16),
            jax.ShapeDtypeStruct((t // TM, AW, TM), jnp.bfloat16),
        ],
        scratch_shapes=[pltpu.VMEM((TM, TN), jnp.float32)],
        compiler_params=pltpu.CompilerParams(
            dimension_semantics=("arbitrary", "arbitrary"),
            vmem_limit_bytes=56 * 1024 * 1024),
        name="inproj",
    )(hn, w_in, wvt, gq, gk, cq, sq, gmat)


def _lam_value(lam_ref, lam_init):
    lv = lam_ref[...]
    return (jnp.exp(jnp.sum(lv[0:1] * lv[1:2], axis=-1, keepdims=True))
            - jnp.exp(jnp.sum(lv[2:3] * lv[3:4], axis=-1, keepdims=True)) + lam_init)


def _finish_heads(o1, o2, lam, lam_init, g_ref, ga_ref, o_ref):
    d = o1 - lam * o2
    dn = d * lax.rsqrt(jnp.mean(d * d, axis=0, keepdims=True) + EPS)
    dn = dn * g_ref[...] * (1.0 - lam_init)
    ga = ga_ref[...].astype(jnp.float32)
    o_ref[...] = (dn.T * (ga * _sigmoid(ga))).astype(o_ref.dtype)


def _split_maps(q_ref):
    qf = q_ref[...].astype(jnp.float32)
    lane = lax.broadcasted_iota(jnp.int32, qf.shape, 1)
    in1 = lane < QK
    return qf, lane, in1, (jnp.where(in1, qf, 0.0), jnp.where(in1, 0.0, qf))


def _online_step(k, vt, qt, m, l, acc):
    st = jnp.dot(k, qt, preferred_element_type=jnp.float32)
    m_new = jnp.maximum(m, jnp.max(st, axis=0, keepdims=True))
    alpha = jnp.exp2(m - m_new)
    p = jnp.exp2(st - m_new)
    l_new = alpha * l + jnp.sum(p, axis=0, keepdims=True)
    acc_new = alpha * acc + jnp.dot(vt, p.astype(jnp.bfloat16), preferred_element_type=jnp.float32)
    return m_new, l_new, acc_new


def _online_init(tq):
    return (jnp.full((1, tq), -jnp.inf, jnp.float32), jnp.zeros((1, tq), jnp.float32),
            jnp.zeros((DV, tq), jnp.float32))


def _attn_ctx_kernel(lam_ref, q_ref, k_ref, vt_ref, ga_ref, g_ref, ag_ref, o_ref, *, lam_init):
    del ag_ref
    _, _, _, qms = _split_maps(q_ref)
    k = k_ref[...]
    vt = vt_ref[0]
    outs = []
    for qm in qms:
        _, l, acc = _online_step(k, vt, qm.T.astype(jnp.bfloat16), *_online_init(CTX))
        outs.append(acc / l)
    _finish_heads(outs[0], outs[1], _lam_value(lam_ref, lam_init), lam_init, g_ref, ga_ref, o_ref)


def _attn_lat_kernel(lam_ref, q_ref, k_ref, vt_ref, ga_ref, g_ref, gmat_ref, o_ref, kmax_ref,
                     *, lam_init, nchunks):
    i = pl.program_id(1)
    f32, bf = jnp.float32, jnp.bfloat16

    @pl.when(i == 0)
    def _():
        def kbody(c, mx):
            kk = k_ref[pl.ds(pl.multiple_of(c * TM, TM), TM), :].astype(f32)
            nrm = jnp.dot((kk * kk).astype(bf), gmat_ref[...], preferred_element_type=f32)
            return jnp.maximum(mx, jnp.max(nrm.reshape(TM // 8, 8, 2 * QK), axis=0))
        mx = lax.fori_loop(0, nchunks, kbody, jnp.zeros((8, 2 * QK), f32))
        kmax_ref[...] = jnp.sqrt(jnp.max(mx, axis=0, keepdims=True)) * BOUND_SLACK

    qf, lane, in1, qms = _split_maps(q_ref)
    qsq = qf * qf
    kmax = kmax_ref[...]
    shifts = (jnp.sqrt(jnp.sum(jnp.where(in1, qsq, 0.0), axis=-1, keepdims=True)) * kmax[:, 0:1],
              jnp.sqrt(jnp.sum(jnp.where(in1, 0.0, qsq), axis=-1, keepdims=True))
              * kmax[:, QK:QK + 1])
    qaug = tuple(jnp.concatenate([qm, jnp.where(lane == 0, -sh, 0.0)], axis=1).T.astype(bf)
                 for qm, sh in zip(qms, shifts))
    shift_max = jnp.maximum(jnp.max(shifts[0]), jnp.max(shifts[1]))
    lam = _lam_value(lam_ref, lam_init)

    def key_rows(c):
        if isinstance(c, int):
            return k_ref[c * TM:(c + 1) * TM, :]
        return k_ref[pl.ds(pl.multiple_of(c * TM, TM), TM), :]

    @pl.when(shift_max <= SHIFT_SAFE)
    def _():
        ones_blk = jnp.ones((TM, V7X_LANES), bf)

        def body(c, stats):
            kaug = jnp.concatenate([key_rows(c), ones_blk], axis=1)
            vt = vt_ref[c]
            sts = [jnp.dot(kaug, qa, preferred_element_type=f32) for qa in qaug]
            ps = [jnp.exp2(st) for st in sts]
            out = []
            for p, (ls, acc) in zip(ps, stats):
                ls = ls + jnp.sum(p.reshape(TM // 8, 8, TQ), axis=0)
                acc = acc + jnp.dot(vt, p.astype(bf), preferred_element_type=f32)
                out.append((ls, acc))
            return tuple(out)

        init = tuple((jnp.zeros((8, TQ), f32), jnp.zeros((DV, TQ), f32)) for _ in range(2))
        (l1, a1), (l2, a2) = lax.fori_loop(0, nchunks, body, init)
        _finish_heads(a1 / jnp.sum(l1, axis=0, keepdims=True),
                      a2 / jnp.sum(l2, axis=0, keepdims=True), lam, lam_init, g_ref, ga_ref, o_ref)

    @pl.when(jnp.logical_not(shift_max <= SHIFT_SAFE))
    def _():
        def body(c, carry):
            k = key_rows(c)
            vt = vt_ref[c]
            return tuple(_online_step(k, vt, qa[0:2 * QK], *st) for qa, st in zip(qaug, carry))

        (_, l1, a1), (_, l2, a2) = lax.fori_loop(0, nchunks, body,
                                                 (_online_init(TQ), _online_init(TQ)))
        _finish_heads(a1 / l1, a2 / l2, lam, lam_init, g_ref, ga_ref, o_ref)


def _attn_call(lamv, z, vt3, g_sub, gmat, lam_init, with_ctx):
    nchunks = vt3.shape[0]
    lat = functools.partial(_attn_lat_kernel, lam_init=lam_init, nchunks=nchunks)
    ag = pl.pallas_call(
        lat,
        grid=(HEADS, SEQ // TQ),
        in_specs=[
            pl.BlockSpec((4, QK), lambda h, i: (0, 0)),
            pl.BlockSpec((TQ, 2 * QK), lambda h, i: (i, ZC_Q // DV + h)),
            pl.BlockSpec((T, 2 * QK), lambda h, i: (0, ZC_K // DV + h)),
            pl.BlockSpec((nchunks, DV, TM), lambda h, i: (0, h, 0)),
            pl.BlockSpec((TQ, DV), lambda h, i: (i, ZC_GA // DV + h)),
            pl.BlockSpec((DV, 1), lambda h, i: (0, 0)),
            pl.BlockSpec((V7X_LANES, V7X_LANES), lambda h, i: (0, 0)),
        ],
        out_specs=pl.BlockSpec((TQ, DV), lambda h, i: (i, h)),
        out_shape=jax.ShapeDtypeStruct((T, AW), jnp.bfloat16),
        scratch_shapes=[pltpu.VMEM((1, 2 * QK), jnp.float32)],
        compiler_params=pltpu.CompilerParams(
            dimension_semantics=("arbitrary", "arbitrary"),
            vmem_limit_bytes=56 * 1024 * 1024),
        name="diff_attn",
    )(lamv, z, z, vt3, z, g_sub, gmat)
    if not with_ctx:
        return ag
    cb = SEQ // CTX
    assert (SEQ % TM) % CTX == 0 and SEQ % TM + CTX == TM
    return pl.pallas_call(
        functools.partial(_attn_ctx_kernel, lam_init=lam_init),
        grid=(HEADS,),
        in_specs=[
            pl.BlockSpec((4, QK), lambda h: (0, 0)),
            pl.BlockSpec((CTX, 2 * QK), lambda h: (cb, ZC_Q // DV + h)),
            pl.BlockSpec((CTX, 2 * QK), lambda h: (cb, ZC_K // DV + h)),
            pl.BlockSpec((1, DV, CTX), lambda h: (SEQ // TM, h, (SEQ % TM) // CTX)),
            pl.BlockSpec((CTX, DV), lambda h: (cb, ZC_GA // DV + h)),
            pl.BlockSpec((DV, 1), lambda h: (0, 0)),
            pl.BlockSpec(memory_space=pl.ANY),
        ],
        out_specs=pl.BlockSpec((CTX, DV), lambda h: (cb, h)),
        out_shape=jax.ShapeDtypeStruct((T, AW), jnp.bfloat16),
        input_output_aliases={6: 0},
        compiler_params=pltpu.CompilerParams(dimension_semantics=("arbitrary",)),
        name="ctx_attn",
    )(lamv, z, z, vt3, z, g_sub, ag)


def _merge_kernel(h_ref, mod_ref, ag_ref, u_ref, v_ref, gb_ref, ma_ref, mb_ref, gv_ref, ws_ref,
                  bs_ref, wpa_ref, wpb_ref, wo_ref, o_ref, bo_ref):
    i = pl.program_id(0)
    f32 = jnp.float32
    for ck in range(TB // CHUNK):
        r0 = ck * CHUNK
        for g in range(GROUPS):
            c0 = g * GDIM
            blk = (slice(r0, r0 + CHUNK), slice(c0, c0 + GDIM))
            v = _gelu_tanh(v_ref[blk].astype(f32))
            vn = v * lax.rsqrt(jnp.mean(v * v, axis=-1, keepdims=True) + EPS) * gv_ref[:, c0:c0 + GDIM]
            s = jnp.dot(ws_ref[g], vn.astype(jnp.bfloat16), preferred_element_type=f32) + bs_ref[g]
            gb = gb_ref[blk].astype(f32)
            b = _gelu_tanh(u_ref[blk].astype(f32)) * s * (gb * _sigmoid(gb))
            bo_ref[blk] = b.astype(bo_ref.dtype)
    ya = jnp.dot(ag_ref[...], wpa_ref[...], preferred_element_type=f32)
    yb = jnp.dot(bo_ref[...], wpb_ref[...], preferred_element_type=f32)
    mix = _sigmoid(ma_ref[...].astype(f32)) * ya + _sigmoid(mb_ref[...].astype(f32)) * yb
    out = jnp.dot(mix.astype(jnp.bfloat16), wo_ref[...], preferred_element_type=jnp.float32)
    gate = jnp.where(i == SEQ // TB, mod_ref[1:2, 2 * D:3 * D], mod_ref[0:1, 2 * D:3 * D])
    o_ref[...] = h_ref[...] + gate * out


def _merge_call(h, mod_l, ag, z, gv, ws, bs, wpa, wpb, wo, rows):
    const = dict(pipeline_mode=pl.Buffered(1))
    return pl.pallas_call(
        _merge_kernel,
        grid=(rows // TB,),
        in_specs=[
            pl.BlockSpec((TB, D), lambda i: (i, 0)),
            pl.BlockSpec((2, 3 * D), lambda i: (0, 0)),
            pl.BlockSpec((TB, AW), lambda i: (i, 0)),
            pl.BlockSpec((TB, BW), lambda i: (i, ZC_U // BW)),
            pl.BlockSpec((TB, BW), lambda i: (i, ZC_VN // BW)),
            pl.BlockSpec((TB, BW), lambda i: (i, ZC_GB // BW)),
            pl.BlockSpec((TB, D), lambda i: (i, ZC_MG // D)),
            pl.BlockSpec((TB, D), lambda i: (i, ZC_MG // D + 1)),
            pl.BlockSpec((1, BW), lambda i: (0, 0)),
            pl.BlockSpec((GROUPS, CHUNK, CHUNK), lambda i: (0, 0, 0)),
            pl.BlockSpec((GROUPS, CHUNK, GDIM), lambda i: (0, 0, 0)),
            pl.BlockSpec((AW, D), lambda i: (0, 0), **const),
            pl.BlockSpec((BW, D), lambda i: (0, 0), **const),
            pl.BlockSpec((D, D), lambda i: (0, 0), **const),
        ],
        out_specs=pl.BlockSpec((TB, D), lambda i: (i, 0)),
        out_shape=jax.ShapeDtypeStruct((rows, D), jnp.float32),
        scratch_shapes=[pltpu.VMEM((TB, BW), jnp.bfloat16)],
        input_output_aliases={0: 0} if rows == T else {},
        compiler_params=pltpu.CompilerParams(
            dimension_semantics=("arbitrary",),
            vmem_limit_bytes=48 * 1024 * 1024),
        name="gmlp_merge",
    )(h, mod_l, ag, z, z, z, z, z, gv, ws, bs, wpa, wpb, wo)


def _rope_tables(seq):
    rows = seq // GRID_W
    row = jnp.repeat(jnp.arange(rows, dtype=jnp.int32), GRID_W).astype(jnp.float32)
    col = jnp.tile(jnp.arange(GRID_W, dtype=jnp.int32), rows).astype(jnp.float32)
    inv = ROPE_BASE ** (-jnp.arange(ROPE_F, dtype=jnp.float32) / ROPE_F)
    ang = jnp.concatenate([row[:, None] * inv, col[:, None] * inv], axis=1)
    cos = jnp.concatenate([jnp.cos(ang), jnp.ones((CTX, 2 * ROPE_F), jnp.float32)], axis=0)
    sin = jnp.concatenate([jnp.sin(ang), jnp.zeros((CTX, 2 * ROPE_F), jnp.float32)], axis=0)
    cr, cc = cos[:, :ROPE_F], cos[:, ROPE_F:]
    sr, sc = sin[:, :ROPE_F], sin[:, ROPE_F:]
    cq = jnp.tile(jnp.concatenate([cr, cr, cc, cc], axis=1), (1, 2))
    sq = jnp.tile(jnp.concatenate([-sr, sr, -sc, sc], axis=1), (1, 2))
    return cq, sq


def kernel(x, c, ctx, c_ctx, w_ada, b_ada, norm_g, w_in, q_norm_g, k_norm_g, lam_q1, lam_k1,
           lam_q2, lam_k2, subln_g, v_norm_g, w_spatial, b_spatial, w_proj_a, w_proj_b, w_out):
    assert x.shape == (1, SEQ, D) and ctx.shape == (1, CTX, D)
    assert T % TM == 0 and SEQ % TQ == 0 and T % TP == 0 and TB == CTX

    bf = jnp.bfloat16
    cq, sq = _rope_tables(SEQ)
    mod = _mod_call(jnp.stack([c[0], c_ctx], axis=1), w_ada, b_ada)
    gmat = (jnp.arange(V7X_LANES)[:, None] // QK == jnp.arange(V7X_LANES)[None, :] // QK).astype(bf)

    nqk = 2 * HEADS * 2 * QK
    w_vt = jnp.swapaxes(w_in[:, :, nqk:nqk + AW], 1, 2).astype(bf)
    ws = w_spatial.astype(bf)
    bs = jnp.broadcast_to(b_spatial[..., None], b_spatial.shape + (GDIM,))
    wpa, wpb, wo = w_proj_a.astype(bf), w_proj_b.astype(bf), w_out.astype(bf)
    gq = jnp.tile(q_norm_g, (1, V7X_LANES // QK))
    gk = jnp.tile(k_norm_g, (1, V7X_LANES // QK))
    lamv = jnp.stack([lam_q1, lam_k1, lam_q2, lam_k2], axis=1)

    h = jnp.concatenate([x[0], ctx[0]], axis=0)
    for l in range(DEPTH):
        lam_init = 0.8 - 0.6 * math.exp(-0.3 * l)
        hn = _prenorm_call(h, mod[l], norm_g[l][None, :])
        z, vt3 = _inproj_call(hn, w_in, l, w_vt[l], gq[l][None, :], gk[l][None, :], cq, sq, gmat)
        last = l == DEPTH - 1
        ag = _attn_call(lamv[l], z, vt3, subln_g[l][:, None], gmat, lam_init, with_ctx=not last)
        h = _merge_call(h, mod[l], ag, z, v_norm_g[l].reshape(1, BW), ws[l], bs[l], wpa[l], wpb[l],
                        wo[l], rows=SEQ if last else T)
    return h[None]
```

```python
import functools
import math

import jax
import jax.numpy as jnp
from jax import lax
from jax.experimental import pallas as pl
from jax.experimental.pallas import tpu as pltpu

D = 2048
DEPTH = 4
GRID_W = 64
SEQ = 4 * GRID_W * GRID_W
CTX = 256
T = SEQ + CTX
HEADS = 8
QK = 64
DV = 128
AW = HEADS * DV
ROPE_F = 16
ROPE_BASE = 10000.0
GROUPS = 8
CHUNK = 128
GDIM = 128
BW = GROUPS * GDIM
EPS = 1e-6
LOG2E = 1.4426950408889634

V7X_LANES = 128
V7X_VMEM_BYTES = 64 * 1024 * 1024

ZC_Q, ZC_K, ZC_GA, ZC_U, ZC_VN, ZC_GB, ZC_MG = 0, 1024, 2048, 3072, 4096, 5120, 6144
ZW = 6144 + 2 * D

SHIFT_SAFE = 50.0
BOUND_SLACK = 1.0 + 2.0 ** -7

TM = 1280
TN = 512
TQ = 1024
UNROLL = 4
TB = 256
GMLP_BLOCKS_PER_YA_SLAB = 2
TP = 640


def _sigmoid(x):
    return 1.0 / (1.0 + jnp.exp(-x))


def _gelu_tanh(x):
    return 0.5 * x * (1.0 + jnp.tanh(math.sqrt(2.0 / math.pi) * (x + 0.044715 * (x * x * x))))


def _mod_kernel(c_ref, w_ref, b_ref, o_ref):
    c = c_ref[...]
    s = c * _sigmoid(c)
    w = w_ref[0]
    b = b_ref[0]
    r0 = jnp.sum(w * s[:, 0:1], axis=0, keepdims=True) + b
    r1 = jnp.sum(w * s[:, 1:2], axis=0, keepdims=True) + b
    o_ref[0] = jnp.concatenate([r0, r1], axis=0)


def _mod_call(c2, w_ada, b_ada):
    tn = 512
    n = w_ada.shape[-1]
    return pl.pallas_call(
        _mod_kernel,
        grid=(DEPTH, n // tn),
        in_specs=[
            pl.BlockSpec((D, 2), lambda l, j: (0, 0)),
            pl.BlockSpec((1, D, tn), lambda l, j: (l, 0, j)),
            pl.BlockSpec((1, 1, tn), lambda l, j: (l, 0, j)),
        ],
        out_specs=pl.BlockSpec((1, 2, tn), lambda l, j: (l, 0, j)),
        out_shape=jax.ShapeDtypeStruct((DEPTH, 2, n), jnp.float32),
        compiler_params=pltpu.CompilerParams(dimension_semantics=("arbitrary", "arbitrary")),
        name="adaln_mod",
    )(c2, w_ada, b_ada.reshape(DEPTH, 1, n))


def _prenorm_kernel(h_ref, mod_ref, g_ref, o_ref):
    i = pl.program_id(0)
    x = h_ref[...]
    rows = i * TP + lax.broadcasted_iota(jnp.int32, (TP, 1), 0)
    is_ctx = rows >= SEQ
    shift = jnp.where(is_ctx, mod_ref[1:2, 0:D], mod_ref[0:1, 0:D])
    scale = jnp.where(is_ctx, mod_ref[1:2, D:2 * D], mod_ref[0:1, D:2 * D])
    y = x * lax.rsqrt(jnp.mean(x * x, axis=-1, keepdims=True) + EPS)
    y = y * g_ref[...]
    o_ref[...] = (y * (1.0 + scale) + shift).astype(o_ref.dtype)


def _prenorm_call(h, mod_l, g):
    t = h.shape[0]
    return pl.pallas_call(
        _prenorm_kernel,
        grid=(t // TP,),
        in_specs=[
            pl.BlockSpec((TP, D), lambda i: (i, 0)),
            pl.BlockSpec((2, 3 * D), lambda i: (0, 0)),
            pl.BlockSpec((1, D), lambda i: (0, 0)),
        ],
        out_specs=pl.BlockSpec((TP, D), lambda i: (i, 0)),
        out_shape=jax.ShapeDtypeStruct((t, D), jnp.bfloat16),
        compiler_params=pltpu.CompilerParams(dimension_semantics=("arbitrary",)),
        name="prenorm",
    )(h, mod_l, g)


J_VT0, J_VT1 = 4, 6
NJ = 2 + ZW // TN


def _inproj_kernel(x_ref, w_ref, wvt_ref, gq_ref, gk_ref, cq_ref, sq_ref, gmat_ref,
                   z_ref, vt_ref, acc_ref):
    j = pl.program_id(1)
    is_vt = (j >= J_VT0) & (j < J_VT1)

    def project():
        return jnp.dot(x_ref[...], w_ref[0].astype(jnp.bfloat16),
                       preferred_element_type=jnp.float32)

    @pl.when(j < J_VT0)
    def _():
        acc_ref[...] = project()

    @pl.when(j >= J_VT1)
    def _():
        z_ref[...] = project().astype(z_ref.dtype)

    def qk_epilogue(gain_ref, out_scale):
        for g in range(TN // V7X_LANES):
            z = acc_ref[:, g * V7X_LANES:(g + 1) * V7X_LANES]
            ss = jnp.dot((z * z).astype(jnp.bfloat16), gmat_ref[...],
                         preferred_element_type=jnp.float32)
            qn = z * lax.rsqrt(ss * (1.0 / QK) + EPS) * gain_ref[...]
            lane = lax.broadcasted_iota(jnp.int32, qn.shape, 1)
            partner = jnp.where((lane % (2 * ROPE_F)) < ROPE_F,
                                pltpu.roll(qn, V7X_LANES - ROPE_F, 1),
                                pltpu.roll(qn, ROPE_F, 1))
            qr = qn * cq_ref[...] + partner * sq_ref[...]
            if out_scale != 1.0:
                qr = qr * out_scale
            z_ref[:, g * V7X_LANES:(g + 1) * V7X_LANES] = qr.astype(z_ref.dtype)

    @pl.when(j < ZC_K // TN)
    def _():
        qk_epilogue(gq_ref, QK ** -0.5 * LOG2E)

    @pl.when((j >= ZC_K // TN) & (j < J_VT0))
    def _():
        qk_epilogue(gk_ref, 1.0)

    @pl.when(is_vt)
    def _():
        vt_ref[0] = lax.dot_general(wvt_ref[...], x_ref[...], (((1,), (1,)), ((), ())),
                                    preferred_element_type=jnp.float32).astype(vt_ref.dtype)


def _inproj_call(hn, w_in, layer, wvt, gq, gk, cq, sq, gmat):
    t = hn.shape[0]

    def wtile(i, j):
        return jnp.where((j >= J_VT0) & (j < J_VT1), J_VT0 - 1, j)

    def ztile(i, j):
        return jnp.where(j < J_VT0, j, jnp.maximum(j - 2, J_VT0 - 1))

    def vtile(i, j):
        return jnp.clip(j - J_VT0, 0, 1)

    return pl.pallas_call(
        _inproj_kernel,
        grid=(t // TM, NJ),
        in_specs=[
            pl.BlockSpec((TM, D), lambda i, j: (i, 0)),
            pl.BlockSpec((1, D, TN), lambda i, j: (layer, 0, wtile(i, j))),
            pl.BlockSpec((TN, D), lambda i, j: (vtile(i, j), 0)),
            pl.BlockSpec((1, V7X_LANES), lambda i, j: (0, 0)),
            pl.BlockSpec((1, V7X_LANES), lambda i, j: (0, 0)),
            pl.BlockSpec((TM, V7X_LANES), lambda i, j: (i, 0)),
            pl.BlockSpec((TM, V7X_LANES), lambda i, j: (i, 0)),
            pl.BlockSpec((V7X_LANES, V7X_LANES), lambda i, j: (0, 0)),
        ],
        out_specs=[
            pl.BlockSpec((TM, TN), lambda i, j: (i, ztile(i, j))),
            pl.BlockSpec((1, TN, TM), lambda i, j: (i, vtile(i, j), 0)),
        ],
        out_shape=[
            jax.ShapeDtypeStruct((t, ZW), jnp.bfloat16),
            jax.ShapeDtypeStruct((t // TM, AW, TM), jnp.bfloat16),
        ],
        scratch_shapes=[pltpu.VMEM((TM, TN), jnp.float32)],
        compiler_params=pltpu.CompilerParams(
            dimension_semantics=("arbitrary", "arbitrary"),
            vmem_limit_bytes=56 * 1024 * 1024),
        name="inproj",
    )(hn, w_in, wvt, gq, gk, cq, sq, gmat)


def _lam_values(lam_ref):
    lv = lam_ref[...]
    lam_init = lv[4:5, 0:1]
    lam = (jnp.exp(jnp.sum(lv[0:1] * lv[1:2], axis=-1, keepdims=True))
           - jnp.exp(jnp.sum(lv[2:3] * lv[3:4], axis=-1, keepdims=True)) + lam_init)
    return lam, lam_init


def _finish_heads(o1, o2, lam, lam_init, g_ref, ga_ref, o_ref):
    d = o1 - lam * o2
    dn = d * lax.rsqrt(jnp.mean(d * d, axis=0, keepdims=True) + EPS)
    dn = dn * g_ref[...] * (1.0 - lam_init)
    ga = ga_ref[...].astype(jnp.float32)
    o_ref[...] = (dn.T * (ga * _sigmoid(ga))).astype(o_ref.dtype)


def _split_maps(q_ref):
    qf = q_ref[...].astype(jnp.float32)
    lane = lax.broadcasted_iota(jnp.int32, qf.shape, 1)
    in1 = lane < QK
    return qf, lane, in1, (jnp.where(in1, qf, 0.0), jnp.where(in1, 0.0, qf))


def _online_step(k, vt, qt, m, l, acc):
    st = jnp.dot(k, qt, preferred_element_type=jnp.float32)
    m_new = jnp.maximum(m, jnp.max(st, axis=0, keepdims=True))
    alpha = jnp.exp2(m - m_new)
    p = jnp.exp2(st - m_new)
    l_new = alpha * l + jnp.sum(p, axis=0, keepdims=True)
    acc_new = alpha * acc + jnp.dot(vt, p.astype(jnp.bfloat16), preferred_element_type=jnp.float32)
    return m_new, l_new, acc_new


def _online_init(tq):
    return (jnp.full((1, tq), -jnp.inf, jnp.float32), jnp.zeros((1, tq), jnp.float32),
            jnp.zeros((DV, tq), jnp.float32))


def _attn_ctx_kernel(lam_ref, q_ref, k_ref, vt_ref, ga_ref, g_ref, ag_ref, o_ref):
    del ag_ref
    _, _, _, qms = _split_maps(q_ref)
    k = k_ref[...]
    vt = vt_ref[0]
    outs = []
    for qm in qms:
        _, l, acc = _online_step(k, vt, qm.T.astype(jnp.bfloat16), *_online_init(CTX))
        outs.append(acc / l)
    _finish_heads(outs[0], outs[1], *_lam_values(lam_ref), g_ref, ga_ref, o_ref)


def _attn_lat_kernel(lam_ref, q_ref, k_ref, vt_ref, ga_ref, g_ref, gmat_ref, o_ref, kmax_ref,
                     *, nchunks):
    i = pl.program_id(1)
    f32, bf = jnp.float32, jnp.bfloat16

    @pl.when(i == 0)
    def _():
        def kbody(c, mx):
            kk = k_ref[pl.ds(pl.multiple_of(c * TM, TM), TM), :].astype(f32)
            nrm = jnp.dot((kk * kk).astype(bf), gmat_ref[...], preferred_element_type=f32)
            return jnp.maximum(mx, jnp.max(nrm.reshape(TM // 8, 8, 2 * QK), axis=0))
        mx = lax.fori_loop(0, nchunks, kbody, jnp.zeros((8, 2 * QK), f32))
        kmax_ref[...] = jnp.sqrt(jnp.max(mx, axis=0, keepdims=True)) * BOUND_SLACK

    qf, lane, in1, qms = _split_maps(q_ref)
    qsq = qf * qf
    kmax = kmax_ref[...]
    shifts = (jnp.sqrt(jnp.sum(jnp.where(in1, qsq, 0.0), axis=-1, keepdims=True)) * kmax[:, 0:1],
              jnp.sqrt(jnp.sum(jnp.where(in1, 0.0, qsq), axis=-1, keepdims=True))
              * kmax[:, QK:QK + 1])
    qaug = tuple(jnp.concatenate([qm, jnp.where(lane == 0, -sh, 0.0)], axis=1).T.astype(bf)
                 for qm, sh in zip(qms, shifts))
    shift_max = jnp.maximum(jnp.max(shifts[0]), jnp.max(shifts[1]))
    lam, lam_init = _lam_values(lam_ref)

    def key_rows(c):
        if isinstance(c, int):
            return k_ref[c * TM:(c + 1) * TM, :]
        return k_ref[pl.ds(pl.multiple_of(c * TM, TM), TM), :]

    @pl.when(shift_max <= SHIFT_SAFE)
    def _():
        ones_blk = jnp.ones((TM, V7X_LANES), bf)

        def body(c, stats):
            kaug = jnp.concatenate([key_rows(c), ones_blk], axis=1)
            vt = vt_ref[c]
            sts = [jnp.dot(kaug, qa, preferred_element_type=f32) for qa in qaug]
            ps = [jnp.exp2(st) for st in sts]
            out = []
            for p, (ls, acc) in zip(ps, stats):
                ls = ls + jnp.sum(p.reshape(TM // 8, 8, TQ), axis=0)
                acc = acc + jnp.dot(vt, p.astype(bf), preferred_element_type=f32)
                out.append((ls, acc))
            return tuple(out)

        init = tuple((jnp.zeros((8, TQ), f32), jnp.zeros((DV, TQ), f32)) for _ in range(2))
        assert (nchunks - 1) % UNROLL == 0

        def step(i, stats):
            for u in range(UNROLL):
                stats = body(1 + UNROLL * i + u, stats)
            return stats

        (l1, a1), (l2, a2) = lax.fori_loop(0, (nchunks - 1) // UNROLL, step, body(0, init))
        _finish_heads(a1 / jnp.sum(l1, axis=0, keepdims=True),
                      a2 / jnp.sum(l2, axis=0, keepdims=True), lam, lam_init, g_ref, ga_ref, o_ref)

    @pl.when(jnp.logical_not(shift_max <= SHIFT_SAFE))
    def _():
        def body(c, carry):
            k = key_rows(c)
            vt = vt_ref[c]
            return tuple(_online_step(k, vt, qa[0:2 * QK], *st) for qa, st in zip(qaug, carry))

        (_, l1, a1), (_, l2, a2) = lax.fori_loop(0, nchunks, body,
                                                 (_online_init(TQ), _online_init(TQ)))
        _finish_heads(a1 / l1, a2 / l2, lam, lam_init, g_ref, ga_ref, o_ref)


def _attn_call(lamv, z, vt3, g_sub, gmat, with_ctx):
    nchunks = vt3.shape[0]
    lat = functools.partial(_attn_lat_kernel, nchunks=nchunks)
    ag = pl.pallas_call(
        lat,
        grid=(HEADS, SEQ // TQ),
        in_specs=[
            pl.BlockSpec((5, QK), lambda h, i: (0, 0)),
            pl.BlockSpec((TQ, 2 * QK), lambda h, i: (i, ZC_Q // DV + h)),
            pl.BlockSpec((T, 2 * QK), lambda h, i: (0, ZC_K // DV + h)),
            pl.BlockSpec((nchunks, DV, TM), lambda h, i: (0, h, 0)),
            pl.BlockSpec((TQ, DV), lambda h, i: (i, ZC_GA // DV + h)),
            pl.BlockSpec((DV, 1), lambda h, i: (0, 0)),
            pl.BlockSpec((V7X_LANES, V7X_LANES), lambda h, i: (0, 0)),
        ],
        out_specs=pl.BlockSpec((TQ, DV), lambda h, i: (i, h)),
        out_shape=jax.ShapeDtypeStruct((T, AW), jnp.bfloat16),
        scratch_shapes=[pltpu.VMEM((1, 2 * QK), jnp.float32)],
        compiler_params=pltpu.CompilerParams(
            dimension_semantics=("arbitrary", "arbitrary"),
            vmem_limit_bytes=56 * 1024 * 1024),
        name="diff_attn",
    )(lamv, z, z, vt3, z, g_sub, gmat)
    if not with_ctx:
        return ag
    cb = SEQ // CTX
    assert (SEQ % TM) % CTX == 0 and SEQ % TM + CTX == TM
    return pl.pallas_call(
        _attn_ctx_kernel,
        grid=(HEADS,),
        in_specs=[
            pl.BlockSpec((5, QK), lambda h: (0, 0)),
            pl.BlockSpec((CTX, 2 * QK), lambda h: (cb, ZC_Q // DV + h)),
            pl.BlockSpec((CTX, 2 * QK), lambda h: (cb, ZC_K // DV + h)),
            pl.BlockSpec((1, DV, CTX), lambda h: (SEQ // TM, h, (SEQ % TM) // CTX)),
            pl.BlockSpec((CTX, DV), lambda h: (cb, ZC_GA // DV + h)),
            pl.BlockSpec((DV, 1), lambda h: (0, 0)),
            pl.BlockSpec(memory_space=pl.ANY),
        ],
        out_specs=pl.BlockSpec((CTX, DV), lambda h: (cb, h)),
        out_shape=jax.ShapeDtypeStruct((T, AW), jnp.bfloat16),
        input_output_aliases={6: 0},
        compiler_params=pltpu.CompilerParams(dimension_semantics=("arbitrary",)),
        name="ctx_attn",
    )(lamv, z, z, vt3, z, g_sub, ag)


def _merge_kernel(h_ref, mod_ref, ag_ref, u_ref, v_ref, gb_ref, ma_ref, mb_ref, gv_ref, ws_ref,
                  bs_ref, wpa_ref, wpb_ref, wo_ref, o_ref, bo_ref):
    i = pl.program_id(0)
    f32 = jnp.float32
    n_blocks = (TB // CHUNK) * GROUPS
    slab = D * GMLP_BLOCKS_PER_YA_SLAB // n_blocks
    ya_slabs = []
    for ck in range(TB // CHUNK):
        r0 = ck * CHUNK
        for g in range(GROUPS):
            idx = ck * GROUPS + g
            if idx % GMLP_BLOCKS_PER_YA_SLAB == 0:
                n0 = (idx // GMLP_BLOCKS_PER_YA_SLAB) * slab
                ya_slabs.append(jnp.dot(ag_ref[...], wpa_ref[:, n0:n0 + slab],
                                        preferred_element_type=f32))
            c0 = g * GDIM
            blk = (slice(r0, r0 + CHUNK), slice(c0, c0 + GDIM))
            v = _gelu_tanh(v_ref[blk].astype(f32))
            vn = v * lax.rsqrt(jnp.mean(v * v, axis=-1, keepdims=True) + EPS) * gv_ref[:, c0:c0 + GDIM]
            s = jnp.dot(ws_ref[g], vn.astype(jnp.bfloat16), preferred_element_type=f32) + bs_ref[g]
            gb = gb_ref[blk].astype(f32)
            b = _gelu_tanh(u_ref[blk].astype(f32)) * s * (gb * _sigmoid(gb))
            bo_ref[blk] = b.astype(bo_ref.dtype)
    ya = jnp.concatenate(ya_slabs, axis=1)
    yb = jnp.dot(bo_ref[...], wpb_ref[...], preferred_element_type=f32)
    mix =_sigmoid(ma_ref[...].astype(f32)) * ya + _sigmoid(mb_ref[...].astype(f32)) * yb
    out = jnp.dot(mix.astype(jnp.bfloat16), wo_ref[...], preferred_element_type=jnp.float32)
    gate = jnp.where(i == SEQ // TB, mod_ref[1:2, 2 * D:3 * D], mod_ref[0:1, 2 * D:3 * D])
    o_ref[...] = h_ref[...] + gate * out


def _merge_call(h, mod_l, ag, z, gv, ws, bs, wpa, wpb, wo, rows):
    const = dict(pipeline_mode=pl.Buffered(1))
    return pl.pallas_call(
        _merge_kernel,
        grid=(rows // TB,),
        in_specs=[
            pl.BlockSpec((TB, D), lambda i: (i, 0)),
            pl.BlockSpec((2, 3 * D), lambda i: (0, 0)),
            pl.BlockSpec((TB, AW), lambda i: (i, 0)),
            pl.BlockSpec((TB, BW), lambda i: (i, ZC_U // BW)),
            pl.BlockSpec((TB, BW), lambda i: (i, ZC_VN // BW)),
            pl.BlockSpec((TB, BW), lambda i: (i, ZC_GB // BW)),
            pl.BlockSpec((TB, D), lambda i: (i, ZC_MG // D)),
            pl.BlockSpec((TB, D), lambda i: (i, ZC_MG // D + 1)),
            pl.BlockSpec((1, BW), lambda i: (0, 0)),
            pl.BlockSpec((GROUPS, CHUNK, CHUNK), lambda i: (0, 0, 0)),
            pl.BlockSpec((GROUPS, CHUNK, GDIM), lambda i: (0, 0, 0)),
            pl.BlockSpec((AW, D), lambda i: (0, 0), **const),
            pl.BlockSpec((BW, D), lambda i: (0, 0), **const),
            pl.BlockSpec((D, D), lambda i: (0, 0), **const),
        ],
        out_specs=pl.BlockSpec((TB, D), lambda i: (i, 0)),
        out_shape=jax.ShapeDtypeStruct((rows, D), jnp.float32),
        scratch_shapes=[pltpu.VMEM((TB, BW), jnp.bfloat16)],
        input_output_aliases={0: 0} if rows == T else {},
        compiler_params=pltpu.CompilerParams(
            dimension_semantics=("arbitrary",),
            vmem_limit_bytes=48 * 1024 * 1024),
        name="gmlp_merge",
    )(h, mod_l, ag, z, z, z, z, z, gv, ws, bs, wpa, wpb, wo)


def _rope_tables(seq):
    rows = seq // GRID_W
    row = jnp.repeat(jnp.arange(rows, dtype=jnp.int32), GRID_W).astype(jnp.float32)
    col = jnp.tile(jnp.arange(GRID_W, dtype=jnp.int32), rows).astype(jnp.float32)
    inv = ROPE_BASE ** (-jnp.arange(ROPE_F, dtype=jnp.float32) / ROPE_F)
    ang = jnp.concatenate([row[:, None] * inv, col[:, None] * inv], axis=1)
    cos = jnp.concatenate([jnp.cos(ang), jnp.ones((CTX, 2 * ROPE_F), jnp.float32)], axis=0)
    sin = jnp.concatenate([jnp.sin(ang), jnp.zeros((CTX, 2 * ROPE_F), jnp.float32)], axis=0)
    cr, cc = cos[:, :ROPE_F], cos[:, ROPE_F:]
    sr, sc = sin[:, :ROPE_F], sin[:, ROPE_F:]
    cq = jnp.tile(jnp.concatenate([cr, cr, cc, cc], axis=1), (1, 2))
    sq = jnp.tile(jnp.concatenate([-sr, sr, -sc, sc], axis=1), (1, 2))
    return cq, sq


def kernel(x, c, ctx, c_ctx, w_ada, b_ada, norm_g, w_in, q_norm_g, k_norm_g, lam_q1, lam_k1,
           lam_q2, lam_k2, subln_g, v_norm_g, w_spatial, b_spatial, w_proj_a, w_proj_b, w_out):
    assert x.shape == (1, SEQ, D) and ctx.shape == (1, CTX, D)
    assert T % TM == 0 and SEQ % TQ == 0 and T % TP == 0 and TB == CTX

    bf = jnp.bfloat16
    cq, sq = _rope_tables(SEQ)
    mod = _mod_call(jnp.stack([c[0], c_ctx], axis=1), w_ada, b_ada)
    gmat = (jnp.arange(V7X_LANES)[:, None] // QK == jnp.arange(V7X_LANES)[None, :] // QK).astype(bf)

    nqk = 2 * HEADS * 2 * QK
    w_vt = jnp.swapaxes(w_in[:, :, nqk:nqk + AW], 1, 2).astype(bf)
    ws = w_spatial.astype(bf)
    bs = jnp.broadcast_to(b_spatial[..., None], b_spatial.shape + (GDIM,))
    wpa, wpb, wo = w_proj_a.astype(bf), w_proj_b.astype(bf), w_out.astype(bf)
    gq = jnp.tile(q_norm_g, (1, V7X_LANES // QK))
    gk = jnp.tile(k_norm_g, (1, V7X_LANES // QK))
    lam_init = jnp.asarray([0.8 - 0.6 * math.exp(-0.3 * l) for l in range(DEPTH)], jnp.float32)
    lamv = jnp.stack([lam_q1, lam_k1, lam_q2, lam_k2,
                      jnp.broadcast_to(lam_init[:, None], lam_q1.shape)], axis=1)

    h = jnp.concatenate([x[0], ctx[0]], axis=0)
    for l in range(DEPTH):
        hn = _prenorm_call(h, mod[l], norm_g[l][None, :])
        z, vt3 = _inproj_call(hn, w_in, l, w_vt[l], gq[l][None, :], gk[l][None, :], cq, sq, gmat)
        last = l == DEPTH - 1
        ag = _attn_call(lamv[l], z, vt3, subln_g[l][:, None], gmat, with_ctx=not last)
        h = _merge_call(h, mod[l], ag, z, v_norm_g[l].reshape(1, BW), ws[l], bs[l], wpa[l], wpb[l],
                        wo[l], rows=SEQ if last else T)
    return h[None]
```

```python
import functools
import math

import jax
import jax.numpy as jnp
from jax import lax
from jax.experimental import pallas as pl
from jax.experimental.pallas import tpu as pltpu

D = 2048
DEPTH = 4
GRID_W = 64
SEQ = 4 * GRID_W * GRID_W
CTX = 256
T = SEQ + CTX
HEADS = 8
QK = 64
DV = 128
AW = HEADS * DV
ROPE_F = 16
ROPE_BASE = 10000.0
GROUPS = 8
CHUNK = 128
GDIM = 128
BW = GROUPS * GDIM
EPS = 1e-6
LOG2E = 1.4426950408889634

V7X_LANES = 128
V7X_VMEM_BYTES = 64 * 1024 * 1024

ZC_K, ZC_GA, ZC_MG, ZC_U, ZC_VN, ZC_GB = 0, 1024, 2048, 6144, 7168, 8192
ZW = ZC_GB + 1024

SHIFT_SAFE = 50.0
BOUND_SLACK = 1.0 + 2.0 ** -7

TM = 1280
TN = 512
TQ = 1024
UNROLL = 4
TB = 256
GMLP_BLOCKS_PER_YA_SLAB = 2
TP = 640


def _sigmoid(x):
    return 1.0 / (1.0 + jnp.exp(-x))


def _gelu_tanh(x):
    return 0.5 * x * (1.0 + jnp.tanh(math.sqrt(2.0 / math.pi) * (x + 0.044715 * (x * x * x))))


def _mod_kernel(c_ref, w_ref, b_ref, o_ref):
    c = c_ref[...]
    s = c * _sigmoid(c)
    w = w_ref[0]
    b = b_ref[0]
    r0 = jnp.sum(w * s[:, 0:1], axis=0, keepdims=True) + b
    r1 = jnp.sum(w * s[:, 1:2], axis=0, keepdims=True) + b
    o_ref[0] = jnp.concatenate([r0, r1], axis=0)


def _mod_call(c2, w_ada, b_ada):
    tn = 512
    n = w_ada.shape[-1]
    return pl.pallas_call(
        _mod_kernel,
        grid=(DEPTH, n // tn),
        in_specs=[
            pl.BlockSpec((D, 2), lambda l, j: (0, 0)),
            pl.BlockSpec((1, D, tn), lambda l, j: (l, 0, j)),
            pl.BlockSpec((1, 1, tn), lambda l, j: (l, 0, j)),
        ],
        out_specs=pl.BlockSpec((1, 2, tn), lambda l, j: (l, 0, j)),
        out_shape=jax.ShapeDtypeStruct((DEPTH, 2, n), jnp.float32),
        compiler_params=pltpu.CompilerParams(dimension_semantics=("arbitrary", "arbitrary")),
        name="adaln_mod",
    )(c2, w_ada, b_ada.reshape(DEPTH, 1, n))


def _prenorm_kernel(h_ref, mod_ref, g_ref, o_ref):
    i = pl.program_id(0)
    x = h_ref[...]
    rows = i * TP + lax.broadcasted_iota(jnp.int32, (TP, 1), 0)
    is_ctx = rows >= SEQ
    shift = jnp.where(is_ctx, mod_ref[1:2, 0:D], mod_ref[0:1, 0:D])
    scale = jnp.where(is_ctx, mod_ref[1:2, D:2 * D], mod_ref[0:1, D:2 * D])
    y = x * lax.rsqrt(jnp.mean(x * x, axis=-1, keepdims=True) + EPS)
    y = y * g_ref[...]
    o_ref[...] = (y * (1.0 + scale) + shift).astype(o_ref.dtype)


def _prenorm_call(h, mod_l, g):
    t = h.shape[0]
    return pl.pallas_call(
        _prenorm_kernel,
        grid=(t // TP,),
        in_specs=[
            pl.BlockSpec((TP, D), lambda i: (i, 0)),
            pl.BlockSpec((2, 3 * D), lambda i: (0, 0)),
            pl.BlockSpec((1, D), lambda i: (0, 0)),
        ],
        out_specs=pl.BlockSpec((TP, D), lambda i: (i, 0)),
        out_shape=jax.ShapeDtypeStruct((t, D), jnp.bfloat16),
        compiler_params=pltpu.CompilerParams(dimension_semantics=("arbitrary",)),
        name="prenorm",
    )(h, mod_l, g)


J_K0, J_VT0, J_VT1 = 2, 4, 6
NJ = J_VT1 + (ZW - ZC_GA) // TN
W_K0 = 2 * HEADS * QK // TN
W_U0 = J_VT1 + AW // TN
W_MG0 = W_U0 + 3 * BW // TN


def _inproj_kernel(x_ref, w_ref, wt_ref, gq_ref, gk_ref, cq_ref, sq_ref, ct_ref, st_ref, gmat_ref,
                   z_ref, qt_ref, vt_ref, acc_ref):
    j = pl.program_id(1)

    def project():
        return jnp.dot(x_ref[...], w_ref[0].astype(jnp.bfloat16),
                       preferred_element_type=jnp.float32)

    def project_t():
        return lax.dot_general(wt_ref[...], x_ref[...], (((1,), (1,)), ((), ())),
                               preferred_element_type=jnp.float32)

    @pl.when(j < J_K0)
    def _():
        zt = project_t()
        gq = gq_ref[...]
        for hm in range(TN // QK):
            blk = zt[hm * QK:(hm + 1) * QK]
            inv = lax.rsqrt(jnp.sum(blk * blk, axis=0, keepdims=True) * (1.0 / QK) + EPS)
            qn = blk * inv * gq
            for ax in range(2):
                x1 = qn[ax * 2 * ROPE_F:ax * 2 * ROPE_F + ROPE_F]
                x2 = qn[ax * 2 * ROPE_F + ROPE_F:(ax + 1) * 2 * ROPE_F]
                cs = ct_ref[ax * ROPE_F:(ax + 1) * ROPE_F]
                sn = st_ref[ax * ROPE_F:(ax + 1) * ROPE_F]
                base = hm * QK + ax * 2 * ROPE_F
                qt_ref[base:base + ROPE_F, :] = (
                    (x1 * cs - x2 * sn) * (QK ** -0.5 * LOG2E)).astype(qt_ref.dtype)
                qt_ref[base + ROPE_F:base + 2 * ROPE_F, :] = (
                    (x2 * cs + x1 * sn) * (QK ** -0.5 * LOG2E)).astype(qt_ref.dtype)

    @pl.when((j >= J_K0) & (j < J_VT0))
    def _():
        acc_ref[...] = project()
        for g in range(TN // V7X_LANES):
            z = acc_ref[:, g * V7X_LANES:(g + 1) * V7X_LANES]
            ss = jnp.dot((z * z).astype(jnp.bfloat16), gmat_ref[...],
                         preferred_element_type=jnp.float32)
            kn = z * lax.rsqrt(ss * (1.0 / QK) + EPS) * gk_ref[...]
            lane = lax.broadcasted_iota(jnp.int32, kn.shape, 1)
            partner = jnp.where((lane % (2 * ROPE_F)) < ROPE_F,
                                pltpu.roll(kn, V7X_LANES - ROPE_F, 1),
                                pltpu.roll(kn, ROPE_F, 1))
            z_ref[:, g * V7X_LANES:(g + 1) * V7X_LANES] = (
                kn * cq_ref[...] + partner * sq_ref[...]).astype(z_ref.dtype)

    @pl.when((j >= J_VT0) & (j < J_VT1))
    def _():
        vt_ref[0] = project_t().astype(vt_ref.dtype)

    @pl.when(j >= J_VT1)
    def _():
        z_ref[...] = project().astype(z_ref.dtype)


def _inproj_call(hn, w_in, layer, wt, gq, gk, cq, sq, ct, st, gmat):
    t = hn.shape[0]

    def wtile(i, j):
        return jnp.where(j < J_K0, W_K0, jnp.where((j >= J_VT0) & (j < J_VT1), W_K0 + 1, j))

    def ztile(i, j):
        head = jnp.where(j < J_VT1, jnp.clip(j - J_K0, 0, 1), j - J_VT1 + ZC_GA // TN)
        return jnp.where(j < W_U0, head,
                         jnp.where(j < W_MG0, j - W_U0 + ZC_U // TN, j - W_MG0 + ZC_MG // TN))

    def ttile(i, j):
        return jnp.where(j < J_K0, j, jnp.clip(j - J_K0, 1, 3))

    return pl.pallas_call(
        _inproj_kernel,
        grid=(t // TM, NJ),
        in_specs=[
            pl.BlockSpec((TM, D), lambda i, j: (i, 0)),
            pl.BlockSpec((1, D, TN), lambda i, j: (layer, 0, wtile(i, j))),
            pl.BlockSpec((TN, D), lambda i, j: (ttile(i, j), 0)),
            pl.BlockSpec((QK, 1), lambda i, j: (0, 0)),
            pl.BlockSpec((1, V7X_LANES), lambda i, j: (0, 0)),
            pl.BlockSpec((TM, V7X_LANES), lambda i, j: (i, 0)),
            pl.BlockSpec((TM, V7X_LANES), lambda i, j: (i, 0)),
            pl.BlockSpec((2 * ROPE_F, TM), lambda i, j: (0, i)),
            pl.BlockSpec((2 * ROPE_F, TM), lambda i, j: (0, i)),
            pl.BlockSpec((V7X_LANES, V7X_LANES), lambda i, j: (0, 0)),
        ],
        out_specs=[
            pl.BlockSpec((TM, TN), lambda i, j: (i, ztile(i, j))),
            pl.BlockSpec((TN, TM), lambda i, j: (jnp.clip(j, 0, 1), i)),
            pl.BlockSpec((1, TN, TM), lambda i, j: (i, jnp.clip(j - J_VT0, 0, 1), 0)),
        ],
        out_shape=[
            jax.ShapeDtypeStruct((t, ZW), jnp.bfloat16),
            jax.ShapeDtypeStruct((AW, t), jnp.bfloat16),
            jax.ShapeDtypeStruct((t // TM, AW, TM), jnp.bfloat16),
        ],
        scratch_shapes=[pltpu.VMEM((TM, TN), jnp.float32)],
        compiler_params=pltpu.CompilerParams(
            dimension_semantics=("arbitrary", "arbitrary"),
            vmem_limit_bytes=56 * 1024 * 1024),
        name="inproj",
    )(hn, w_in, wt, gq, gk, cq, sq, ct, st, gmat)


def _lam_values(lam_ref):
    lv = lam_ref[...]
    lam_init = lv[4:5, 0:1]
    lam = (jnp.exp(jnp.sum(lv[0:1] * lv[1:2], axis=-1, keepdims=True))
           - jnp.exp(jnp.sum(lv[2:3] * lv[3:4], axis=-1, keepdims=True)) + lam_init)
    return lam, lam_init


def _finish_heads(o1, o2, lam, lam_init, g_ref, ga_ref, o_ref):
    d = o1 - lam * o2
    dn = d * lax.rsqrt(jnp.mean(d * d, axis=0, keepdims=True) + EPS)
    dn = dn * g_ref[...] * (1.0 - lam_init)
    ga = ga_ref[...].astype(jnp.float32)
    o_ref[...] = (dn.T * (ga * _sigmoid(ga))).astype(o_ref.dtype)


def _split_maps(qt_ref):
    qt = qt_ref[...].astype(jnp.float32)
    row = lax.broadcasted_iota(jnp.int32, qt.shape, 0)
    in1 = row < QK
    return qt, row, in1, (jnp.where(in1, qt, 0.0), jnp.where(in1, 0.0, qt))


def _online_step(k, vt, qt, m, l, acc):
    st = jnp.dot(k, qt, preferred_element_type=jnp.float32)
    m_new = jnp.maximum(m, jnp.max(st, axis=0, keepdims=True))
    alpha = jnp.exp2(m - m_new)
    p = jnp.exp2(st - m_new)
    l_new = alpha * l + jnp.sum(p, axis=0, keepdims=True)
    acc_new = alpha * acc + jnp.dot(vt, p.astype(jnp.bfloat16), preferred_element_type=jnp.float32)
    return m_new, l_new, acc_new


def _online_init(tq):
    return (jnp.full((1, tq), -jnp.inf, jnp.float32), jnp.zeros((1, tq), jnp.float32),
            jnp.zeros((DV, tq), jnp.float32))


def _attn_ctx_kernel(lam_ref, qt_ref, k_ref, vt_ref, ga_ref, g_ref, ag_ref, o_ref):
    del ag_ref
    _, _, _, qms = _split_maps(qt_ref)
    k = k_ref[...]
    vt = vt_ref[0]
    outs = []
    for qm in qms:
        _, l, acc = _online_step(k, vt, qm.astype(jnp.bfloat16), *_online_init(CTX))
        outs.append(acc / l)
    _finish_heads(outs[0], outs[1], *_lam_values(lam_ref), g_ref, ga_ref, o_ref)


def _attn_lat_kernel(lam_ref, qt_ref, k_ref, vt_ref, ga_ref, g_ref, gmat_ref, o_ref, kmax_ref,
                     *, nchunks):
    i = pl.program_id(1)
    f32, bf = jnp.float32, jnp.bfloat16

    @pl.when(i == 0)
    def _():
        def kbody(c, mx):
            kk = k_ref[pl.ds(pl.multiple_of(c * TM, TM), TM), :].astype(f32)
            nrm = jnp.dot((kk * kk).astype(bf), gmat_ref[...], preferred_element_type=f32)
            return jnp.maximum(mx, jnp.max(nrm.reshape(TM // 8, 8, 2 * QK), axis=0))
        mx = lax.fori_loop(0, nchunks, kbody, jnp.zeros((8, 2 * QK), f32))
        kmax_ref[...] = jnp.sqrt(jnp.max(mx, axis=0, keepdims=True)) * BOUND_SLACK

    qt, row, in1, qms = _split_maps(qt_ref)
    qsq = qt * qt
    kmax = kmax_ref[...]
    shifts = (jnp.sqrt(jnp.sum(jnp.where(in1, qsq, 0.0), axis=0, keepdims=True)) * kmax[:, 0:1],
              jnp.sqrt(jnp.sum(jnp.where(in1, 0.0, qsq), axis=0, keepdims=True))
              * kmax[:, QK:QK + 1])
    qaug = tuple(jnp.concatenate([qm, jnp.where(row == 0, -sh, 0.0)], axis=0).astype(bf)
                 for qm, sh in zip(qms, shifts))
    shift_max = jnp.maximum(jnp.max(shifts[0]), jnp.max(shifts[1]))
    lam, lam_init = _lam_values(lam_ref)

    def key_rows(c):
        if isinstance(c, int):
            return k_ref[c * TM:(c + 1) * TM, :]
        return k_ref[pl.ds(pl.multiple_of(c * TM, TM), TM), :]

    @pl.when(shift_max <= SHIFT_SAFE)
    def _():
        ones_blk = jnp.ones((TM, V7X_LANES), bf)

        def body(c, stats):
            kaug = jnp.concatenate([key_rows(c), ones_blk], axis=1)
            vt = vt_ref[c]
            sts = [jnp.dot(kaug, qa, preferred_element_type=f32) for qa in qaug]
            ps = [jnp.exp2(st) for st in sts]
            out = []
            for p, (ls, acc) in zip(ps, stats):
                ls = ls + jnp.sum(p.reshape(TM // 8, 8, TQ), axis=0)
                acc = acc + jnp.dot(vt, p.astype(bf), preferred_element_type=f32)
                out.append((ls, acc))
            return tuple(out)

        init = tuple((jnp.zeros((8, TQ), f32), jnp.zeros((DV, TQ), f32)) for _ in range(2))
        assert (nchunks - 1) % UNROLL == 0

        def step(i, stats):
            for u in range(UNROLL):
                stats = body(1 + UNROLL * i + u, stats)
            return stats

        (l1, a1), (l2, a2) = lax.fori_loop(0, (nchunks - 1) // UNROLL, step, body(0, init))
        _finish_heads(a1 / jnp.sum(l1, axis=0, keepdims=True),
                      a2 / jnp.sum(l2, axis=0, keepdims=True), lam, lam_init, g_ref, ga_ref, o_ref)

    @pl.when(jnp.logical_not(shift_max <= SHIFT_SAFE))
    def _():
        def body(c, carry):
            k = key_rows(c)
            vt = vt_ref[c]
            return tuple(_online_step(k, vt, qa[0:2 * QK], *st) for qa, st in zip(qaug, carry))

        (_, l1, a1), (_, l2, a2) = lax.fori_loop(0, nchunks, body,
                                                 (_online_init(TQ), _online_init(TQ)))
        _finish_heads(a1 / l1, a2 / l2, lam, lam_init, g_ref, ga_ref, o_ref)


def _attn_call(lamv, z, qt, vt3, g_sub, gmat, with_ctx):
    nchunks = vt3.shape[0]
    lat = functools.partial(_attn_lat_kernel, nchunks=nchunks)
    ag = pl.pallas_call(
        lat,
        grid=(HEADS, SEQ // TQ),
        in_specs=[
            pl.BlockSpec((5, QK), lambda h, i: (0, 0)),
            pl.BlockSpec((2 * QK, TQ), lambda h, i: (h, i)),
            pl.BlockSpec((T, 2 * QK), lambda h, i: (0, ZC_K // DV + h)),
            pl.BlockSpec((nchunks, DV, TM), lambda h, i: (0, h, 0)),
            pl.BlockSpec((TQ, DV), lambda h, i: (i, ZC_GA // DV + h)),
            pl.BlockSpec((DV, 1), lambda h, i: (0, 0)),
            pl.BlockSpec((V7X_LANES, V7X_LANES), lambda h, i: (0, 0)),
        ],
        out_specs=pl.BlockSpec((TQ, DV), lambda h, i: (i, h)),
        out_shape=jax.ShapeDtypeStruct((T, AW), jnp.bfloat16),
        scratch_shapes=[pltpu.VMEM((1, 2 * QK), jnp.float32)],
        compiler_params=pltpu.CompilerParams(
            dimension_semantics=("arbitrary", "arbitrary"),
            vmem_limit_bytes=56 * 1024 * 1024),
        name="diff_attn",
    )(lamv, qt, z, vt3, z, g_sub, gmat)
    if not with_ctx:
        return ag
    cb = SEQ // CTX
    assert (SEQ % TM) % CTX == 0 and SEQ % TM + CTX == TM
    return pl.pallas_call(
        _attn_ctx_kernel,
        grid=(HEADS,),
        in_specs=[
            pl.BlockSpec((5, QK), lambda h: (0, 0)),
            pl.BlockSpec((2 * QK, CTX), lambda h: (h, cb)),
            pl.BlockSpec((CTX, 2 * QK), lambda h: (cb, ZC_K // DV + h)),
            pl.BlockSpec((1, DV, CTX), lambda h: (SEQ // TM, h, (SEQ % TM) // CTX)),
            pl.BlockSpec((CTX, DV), lambda h: (cb, ZC_GA // DV + h)),
            pl.BlockSpec((DV, 1), lambda h: (0, 0)),
            pl.BlockSpec(memory_space=pl.ANY),
        ],
        out_specs=pl.BlockSpec((CTX, DV), lambda h: (cb, h)),
        out_shape=jax.ShapeDtypeStruct((T, AW), jnp.bfloat16),
        input_output_aliases={6: 0},
        compiler_params=pltpu.CompilerParams(dimension_semantics=("arbitrary",)),
        name="ctx_attn",
    )(lamv, qt, z, vt3, z, g_sub, ag)


def _merge_kernel(h_ref, mod_ref, ag_ref, u_ref, v_ref, gb_ref, ma_ref, mb_ref, gv_ref, ws_ref,
                  bs_ref, wpa_ref, wpb_ref, wo_ref, o_ref, bo_ref):
    i = pl.program_id(0)
    f32 = jnp.float32
    n_blocks = (TB // CHUNK) * GROUPS
    slab = D * GMLP_BLOCKS_PER_YA_SLAB // n_blocks
    ya_slabs = []
    for ck in range(TB // CHUNK):
        r0 = ck * CHUNK
        for g in range(GROUPS):
            idx = ck * GROUPS + g
            if idx % GMLP_BLOCKS_PER_YA_SLAB == 0:
                n0 = (idx // GMLP_BLOCKS_PER_YA_SLAB) * slab
                ya_slabs.append(jnp.dot(ag_ref[...], wpa_ref[:, n0:n0 + slab],
                                        preferred_element_type=f32))
            c0 = g * GDIM
            blk = (slice(r0, r0 + CHUNK), slice(c0, c0 + GDIM))
            v = _gelu_tanh(v_ref[blk].astype(f32))
            vn = v * lax.rsqrt(jnp.mean(v * v, axis=-1, keepdims=True) + EPS) * gv_ref[:, c0:c0 + GDIM]
            s = jnp.dot(ws_ref[g], vn.astype(jnp.bfloat16), preferred_element_type=f32) + bs_ref[g]
            gb = gb_ref[blk].astype(f32)
            b = _gelu_tanh(u_ref[blk].astype(f32)) * s * (gb * _sigmoid(gb))
            bo_ref[blk] = b.astype(bo_ref.dtype)
    ya = jnp.concatenate(ya_slabs, axis=1)
    yb = jnp.dot(bo_ref[...], wpb_ref[...], preferred_element_type=f32)
    mix =_sigmoid(ma_ref[...].astype(f32)) * ya + _sigmoid(mb_ref[...].astype(f32)) * yb
    out = jnp.dot(mix.astype(jnp.bfloat16), wo_ref[...], preferred_element_type=jnp.float32)
    gate = jnp.where(i == SEQ // TB, mod_ref[1:2, 2 * D:3 * D], mod_ref[0:1, 2 * D:3 * D])
    o_ref[...] = h_ref[...] + gate * out


def _merge_call(h, mod_l, ag, z, gv, ws, bs, wpa, wpb, wo, rows):
    const = dict(pipeline_mode=pl.Buffered(1))
    return pl.pallas_call(
        _merge_kernel,
        grid=(rows // TB,),
        in_specs=[
            pl.BlockSpec((TB, D), lambda i: (i, 0)),
            pl.BlockSpec((2, 3 * D), lambda i: (0, 0)),
            pl.BlockSpec((TB, AW), lambda i: (i, 0)),
            pl.BlockSpec((TB, BW), lambda i: (i, ZC_U // BW)),
            pl.BlockSpec((TB, BW), lambda i: (i, ZC_VN // BW)),
            pl.BlockSpec((TB, BW), lambda i: (i, ZC_GB // BW)),
            pl.BlockSpec((TB, D), lambda i: (i, ZC_MG // D)),
            pl.BlockSpec((TB, D), lambda i: (i, ZC_MG // D + 1)),
            pl.BlockSpec((1, BW), lambda i: (0, 0)),
            pl.BlockSpec((GROUPS, CHUNK, CHUNK), lambda i: (0, 0, 0)),
            pl.BlockSpec((GROUPS, CHUNK, GDIM), lambda i: (0, 0, 0)),
            pl.BlockSpec((AW, D), lambda i: (0, 0), **const),
            pl.BlockSpec((BW, D), lambda i: (0, 0), **const),
            pl.BlockSpec((D, D), lambda i: (0, 0), **const),
        ],
        out_specs=pl.BlockSpec((TB, D), lambda i: (i, 0)),
        out_shape=jax.ShapeDtypeStruct((rows, D), jnp.float32),
        scratch_shapes=[pltpu.VMEM((TB, BW), jnp.bfloat16)],
        input_output_aliases={0: 0} if rows == T else {},
        compiler_params=pltpu.CompilerParams(
            dimension_semantics=("arbitrary",),
            vmem_limit_bytes=48 * 1024 * 1024),
        name="gmlp_merge",
    )(h, mod_l, ag, z, z, z, z, z, gv, ws, bs, wpa, wpb, wo)


def _rope_tables(seq):
    rows = seq // GRID_W
    row = jnp.repeat(jnp.arange(rows, dtype=jnp.int32), GRID_W).astype(jnp.float32)
    col = jnp.tile(jnp.arange(GRID_W, dtype=jnp.int32), rows).astype(jnp.float32)
    inv = ROPE_BASE ** (-jnp.arange(ROPE_F, dtype=jnp.float32) / ROPE_F)
    ang = jnp.concatenate([row[:, None] * inv, col[:, None] * inv], axis=1)
    cos = jnp.concatenate([jnp.cos(ang), jnp.ones((CTX, 2 * ROPE_F), jnp.float32)], axis=0)
    sin = jnp.concatenate([jnp.sin(ang), jnp.zeros((CTX, 2 * ROPE_F), jnp.float32)], axis=0)
    cr, cc = cos[:, :ROPE_F], cos[:, ROPE_F:]
    sr, sc = sin[:, :ROPE_F], sin[:, ROPE_F:]
    cq = jnp.tile(jnp.concatenate([cr, cr, cc, cc], axis=1), (1, 2))
    sq = jnp.tile(jnp.concatenate([-sr, sr, -sc, sc], axis=1), (1, 2))
    return cq, sq, cos.T, sin.T


def kernel(x, c, ctx, c_ctx, w_ada, b_ada, norm_g, w_in, q_norm_g, k_norm_g, lam_q1, lam_k1,
           lam_q2, lam_k2, subln_g, v_norm_g, w_spatial, b_spatial, w_proj_a, w_proj_b, w_out):
    assert x.shape == (1, SEQ, D) and ctx.shape == (1, CTX, D)
    assert T % TM == 0 and SEQ % TQ == 0 and T % TP == 0 and TB == CTX

    bf = jnp.bfloat16
    cq, sq, ct, st = _rope_tables(SEQ)
    mod = _mod_call(jnp.stack([c[0], c_ctx], axis=1), w_ada, b_ada)
    gmat = (jnp.arange(V7X_LANES)[:, None] // QK == jnp.arange(V7X_LANES)[None, :] // QK).astype(bf)

    nq = HEADS * 2 * QK
    w_t = jnp.swapaxes(jnp.concatenate([w_in[:, :, :nq], w_in[:, :, 2 * nq:2 * nq + AW]], axis=-1),
                       1, 2).astype(bf)
    ws = w_spatial.astype(bf)
    bs = jnp.broadcast_to(b_spatial[..., None], b_spatial.shape + (GDIM,))
    wpa, wpb, wo = w_proj_a.astype(bf), w_proj_b.astype(bf), w_out.astype(bf)
    gk = jnp.tile(k_norm_g, (1, V7X_LANES // QK))
    lam_init = jnp.asarray([0.8 - 0.6 * math.exp(-0.3 * l) for l in range(DEPTH)], jnp.float32)
    lamv = jnp.stack([lam_q1, lam_k1, lam_q2, lam_k2,
                      jnp.broadcast_to(lam_init[:, None], lam_q1.shape)], axis=1)

    h = jnp.concatenate([x[0], ctx[0]], axis=0)
    for l in range(DEPTH):
        hn = _prenorm_call(h, mod[l], norm_g[l][None, :])
        z, qt, vt3 = _inproj_call(hn, w_in, l, w_t[l], q_norm_g[l][:, None], gk[l][None, :],
                                  cq, sq, ct, st, gmat)
        last = l == DEPTH - 1
        ag = _attn_call(lamv[l], z, qt, vt3, subln_g[l][:, None], gmat, with_ctx=not last)
        h = _merge_call(h, mod[l], ag, z, v_norm_g[l].reshape(1, BW), ws[l], bs[l], wpa[l], wpb[l],
                        wo[l], rows=SEQ if last else T)
    return h[None]
```

```python
import functools
import math

import jax
import jax.numpy as jnp
from jax import lax
from jax.experimental import pallas as pl
from jax.experimental.pallas import tpu as pltpu

D = 2048
DEPTH = 4
GRID_W = 64
SEQ = 4 * GRID_W * GRID_W
CTX = 256
T = SEQ + CTX
HEADS = 8
QK = 64
DV = 128
AW = HEADS * DV
ROPE_F = 16
ROPE_BASE = 10000.0
GROUPS = 8
CHUNK = 128
GDIM = 128
BW = GROUPS * GDIM
EPS = 1e-6
LOG2E = 1.4426950408889634

V7X_LANES = 128
V7X_VMEM_BYTES = 64 * 1024 * 1024

ZC_K, ZC_GA, ZC_MG, ZC_U, ZC_VN, ZC_GB = 0, 1024, 2048, 6144, 7168, 8192
ZW = ZC_GB + 1024

SHIFT_SAFE = 50.0
BOUND_SLACK = 1.0 + 2.0 ** -7

TM = 1280
TN = 512
TQ = 1024
UNROLL = 4
TB = 256
GMLP_BLOCKS_PER_YA_SLAB = 2
TP = 640


def _sigmoid(x):
    return 1.0 / (1.0 + jnp.exp(-x))


def _gelu_tanh(x):
    return 0.5 * x * (1.0 + jnp.tanh(math.sqrt(2.0 / math.pi) * (x + 0.044715 * (x * x * x))))


def _mod_kernel(c_ref, w_ref, b_ref, o_ref):
    c = c_ref[...]
    s = c * _sigmoid(c)
    w = w_ref[0]
    b = b_ref[0]
    r0 = jnp.sum(w * s[:, 0:1], axis=0, keepdims=True) + b
    r1 = jnp.sum(w * s[:, 1:2], axis=0, keepdims=True) + b
    o_ref[0] = jnp.concatenate([r0, r1], axis=0)


def _mod_call(c2, w_ada, b_ada):
    tn = 512
    n = w_ada.shape[-1]
    return pl.pallas_call(
        _mod_kernel,
        grid=(DEPTH, n // tn),
        in_specs=[
            pl.BlockSpec((D, 2), lambda l, j: (0, 0)),
            pl.BlockSpec((1, D, tn), lambda l, j: (l, 0, j)),
            pl.BlockSpec((1, 1, tn), lambda l, j: (l, 0, j)),
        ],
        out_specs=pl.BlockSpec((1, 2, tn), lambda l, j: (l, 0, j)),
        out_shape=jax.ShapeDtypeStruct((DEPTH, 2, n), jnp.float32),
        compiler_params=pltpu.CompilerParams(dimension_semantics=("arbitrary", "arbitrary")),
        name="adaln_mod",
    )(c2, w_ada, b_ada.reshape(DEPTH, 1, n))


def _prenorm_kernel(h_ref, mod_ref, g_ref, o_ref):
    i = pl.program_id(0)
    x = h_ref[...]
    rows = i * TP + lax.broadcasted_iota(jnp.int32, (TP, 1), 0)
    is_ctx = rows >= SEQ
    shift = jnp.where(is_ctx, mod_ref[1:2, 0:D], mod_ref[0:1, 0:D])
    scale = jnp.where(is_ctx, mod_ref[1:2, D:2 * D], mod_ref[0:1, D:2 * D])
    y = x * lax.rsqrt(jnp.mean(x * x, axis=-1, keepdims=True) + EPS)
    y = y * g_ref[...]
    o_ref[...] = (y * (1.0 + scale) + shift).astype(o_ref.dtype)


def _prenorm_call(h, mod_l, g):
    t = h.shape[0]
    return pl.pallas_call(
        _prenorm_kernel,
        grid=(t // TP,),
        in_specs=[
            pl.BlockSpec((TP, D), lambda i: (i, 0)),
            pl.BlockSpec((2, 3 * D), lambda i: (0, 0)),
            pl.BlockSpec((1, D), lambda i: (0, 0)),
        ],
        out_specs=pl.BlockSpec((TP, D), lambda i: (i, 0)),
        out_shape=jax.ShapeDtypeStruct((t, D), jnp.bfloat16),
        compiler_params=pltpu.CompilerParams(dimension_semantics=("arbitrary",)),
        name="prenorm",
    )(h, mod_l, g)


J_K0, J_VT0, J_VT1 = 2, 4, 6
NJ = J_VT1 + (ZW - ZC_GA) // TN
W_K0 = 2 * HEADS * QK // TN
W_U0 = J_VT1 + AW // TN
W_MG0 = W_U0 + 3 * BW // TN


def _inproj_kernel(x_ref, w_ref, wqt_ref, wvt_ref, gq_ref, gk_ref, cq_ref, sq_ref, ct_ref, st_ref,
                   gmat_ref, z_ref, qt_ref, vt_ref, acc_ref):
    j = pl.program_id(1)

    def project():
        return jnp.dot(x_ref[...], w_ref[0].astype(jnp.bfloat16),
                       preferred_element_type=jnp.float32)

    def project_t(wt_ref):
        return lax.dot_general(wt_ref[...], x_ref[...], (((1,), (1,)), ((), ())),
                               preferred_element_type=jnp.float32)

    @pl.when(j < J_K0)
    def _():
        zt = project_t(wqt_ref)
        gq = gq_ref[...]
        for hm in range(TN // QK):
            blk = zt[hm * QK:(hm + 1) * QK]
            inv = lax.rsqrt(jnp.sum(blk * blk, axis=0, keepdims=True) * (1.0 / QK) + EPS)
            qn = blk * inv * gq
            for ax in range(2):
                x1 = qn[ax * 2 * ROPE_F:ax * 2 * ROPE_F + ROPE_F]
                x2 = qn[ax * 2 * ROPE_F + ROPE_F:(ax + 1) * 2 * ROPE_F]
                cs = ct_ref[ax * ROPE_F:(ax + 1) * ROPE_F]
                sn = st_ref[ax * ROPE_F:(ax + 1) * ROPE_F]
                base = hm * QK + ax * 2 * ROPE_F
                qt_ref[base:base + ROPE_F, :] = (
                    (x1 * cs - x2 * sn) * (QK ** -0.5 * LOG2E)).astype(qt_ref.dtype)
                qt_ref[base + ROPE_F:base + 2 * ROPE_F, :] = (
                    (x2 * cs + x1 * sn) * (QK ** -0.5 * LOG2E)).astype(qt_ref.dtype)

    @pl.when((j >= J_K0) & (j < J_VT0))
    def _():
        acc_ref[...] = project()
        for g in range(TN // V7X_LANES):
            z = acc_ref[:, g * V7X_LANES:(g + 1) * V7X_LANES]
            ss = jnp.dot((z * z).astype(jnp.bfloat16), gmat_ref[...],
                         preferred_element_type=jnp.float32)
            kn = z * lax.rsqrt(ss * (1.0 / QK) + EPS) * gk_ref[...]
            lane = lax.broadcasted_iota(jnp.int32, kn.shape, 1)
            partner = jnp.where((lane % (2 * ROPE_F)) < ROPE_F,
                                pltpu.roll(kn, V7X_LANES - ROPE_F, 1),
                                pltpu.roll(kn, ROPE_F, 1))
            z_ref[:, g * V7X_LANES:(g + 1) * V7X_LANES] = (
                kn * cq_ref[...] + partner * sq_ref[...]).astype(z_ref.dtype)

    @pl.when((j >= J_VT0) & (j < J_VT1))
    def _():
        vt_ref[0] = project_t(wvt_ref).astype(vt_ref.dtype)

    @pl.when(j >= J_VT1)
    def _():
        z_ref[...] = project().astype(z_ref.dtype)


def _inproj_call(hn, w_in, layer, wqt, wvt, gq, gk, cq, sq, ct, st, gmat):
    t = hn.shape[0]

    def wtile(i, j):
        return jnp.where(j < J_K0, W_K0, jnp.where((j >= J_VT0) & (j < J_VT1), W_K0 + 1, j))

    def ztile(i, j):
        head = jnp.where(j < J_VT1, jnp.clip(j - J_K0, 0, 1), j - J_VT1 + ZC_GA // TN)
        return jnp.where(j < W_U0, head,
                         jnp.where(j < W_MG0, j - W_U0 + ZC_U // TN, j - W_MG0 + ZC_MG // TN))

    return pl.pallas_call(
        _inproj_kernel,
        grid=(t // TM, NJ),
        in_specs=[
            pl.BlockSpec((TM, D), lambda i, j: (i, 0)),
            pl.BlockSpec((1, D, TN), lambda i, j: (layer, 0, wtile(i, j))),
            pl.BlockSpec((TN, D), lambda i, j: (jnp.clip(j, 0, 1), 0)),
            pl.BlockSpec((TN, D), lambda i, j: (jnp.clip(j - J_VT0, 0, 1), 0)),
            pl.BlockSpec((QK, 1), lambda i, j: (0, 0)),
            pl.BlockSpec((1, V7X_LANES), lambda i, j: (0, 0)),
            pl.BlockSpec((TM, V7X_LANES), lambda i, j: (i, 0)),
            pl.BlockSpec((TM, V7X_LANES), lambda i, j: (i, 0)),
            pl.BlockSpec((2 * ROPE_F, TM), lambda i, j: (0, i)),
            pl.BlockSpec((2 * ROPE_F, TM), lambda i, j: (0, i)),
            pl.BlockSpec((V7X_LANES, V7X_LANES), lambda i, j: (0, 0)),
        ],
        out_specs=[
            pl.BlockSpec((TM, TN), lambda i, j: (i, ztile(i, j))),
            pl.BlockSpec((TN, TM), lambda i, j: (jnp.clip(j, 0, 1), i)),
            pl.BlockSpec((1, TN, TM), lambda i, j: (i, jnp.clip(j - J_VT0, 0, 1), 0)),
        ],
        out_shape=[
            jax.ShapeDtypeStruct((t, ZW), jnp.bfloat16),
            jax.ShapeDtypeStruct((AW, t), jnp.bfloat16),
            jax.ShapeDtypeStruct((t // TM, AW, TM), jnp.bfloat16),
        ],
        scratch_shapes=[pltpu.VMEM((TM, TN), jnp.float32)],
        compiler_params=pltpu.CompilerParams(
            dimension_semantics=("arbitrary", "arbitrary"),
            vmem_limit_bytes=56 * 1024 * 1024),
        name="inproj",
    )(hn, w_in, wqt, wvt, gq, gk, cq, sq, ct, st, gmat)


def _lam_values(lam_ref):
    lv = lam_ref[...]
    lam_init = lv[4:5, 0:1]
    lam = (jnp.exp(jnp.sum(lv[0:1] * lv[1:2], axis=-1, keepdims=True))
           - jnp.exp(jnp.sum(lv[2:3] * lv[3:4], axis=-1, keepdims=True)) + lam_init)
    return lam, lam_init


def _finish_heads(o1, o2, lam, lam_init, g_ref, ga_ref, o_ref):
    d = o1 - lam * o2
    dn = d * lax.rsqrt(jnp.mean(d * d, axis=0, keepdims=True) + EPS)
    dn = dn * g_ref[...] * (1.0 - lam_init)
    ga = ga_ref[...].astype(jnp.float32)
    o_ref[...] = (dn.T * (ga * _sigmoid(ga))).astype(o_ref.dtype)


def _split_maps(qt_ref):
    qt = qt_ref[...].astype(jnp.float32)
    row = lax.broadcasted_iota(jnp.int32, qt.shape, 0)
    in1 = row < QK
    return qt, row, in1, (jnp.where(in1, qt, 0.0), jnp.where(in1, 0.0, qt))


def _online_step(k, vt, qt, m, l, acc):
    st = jnp.dot(k, qt, preferred_element_type=jnp.float32)
    m_new = jnp.maximum(m, jnp.max(st, axis=0, keepdims=True))
    alpha = jnp.exp2(m - m_new)
    p = jnp.exp2(st - m_new)
    l_new = alpha * l + jnp.sum(p, axis=0, keepdims=True)
    acc_new = alpha * acc + jnp.dot(vt, p.astype(jnp.bfloat16), preferred_element_type=jnp.float32)
    return m_new, l_new, acc_new


def _online_init(tq):
    return (jnp.full((1, tq), -jnp.inf, jnp.float32), jnp.zeros((1, tq), jnp.float32),
            jnp.zeros((DV, tq), jnp.float32))


def _attn_ctx_kernel(lam_ref, qt_ref, k_ref, vt_ref, ga_ref, g_ref, ag_ref, o_ref):
    del ag_ref
    _, _, _, qms = _split_maps(qt_ref)
    k = k_ref[...]
    vt = vt_ref[0]
    outs = []
    for qm in qms:
        _, l, acc = _online_step(k, vt, qm.astype(jnp.bfloat16), *_online_init(CTX))
        outs.append(acc / l)
    _finish_heads(outs[0], outs[1], *_lam_values(lam_ref), g_ref, ga_ref, o_ref)


def _attn_lat_kernel(lam_ref, qt_ref, k_ref, vt_ref, ga_ref, g_ref, gmat_ref, o_ref, kmax_ref,
                     *, nchunks):
    i = pl.program_id(1)
    f32, bf = jnp.float32, jnp.bfloat16

    @pl.when(i == 0)
    def _():
        def kbody(c, mx):
            kk = k_ref[pl.ds(pl.multiple_of(c * TM, TM), TM), :].astype(f32)
            nrm = jnp.dot((kk * kk).astype(bf), gmat_ref[...], preferred_element_type=f32)
            return jnp.maximum(mx, jnp.max(nrm.reshape(TM // 8, 8, 2 * QK), axis=0))
        mx = lax.fori_loop(0, nchunks, kbody, jnp.zeros((8, 2 * QK), f32))
        kmax_ref[...] = jnp.sqrt(jnp.max(mx, axis=0, keepdims=True)) * BOUND_SLACK

    qt, row, in1, qms = _split_maps(qt_ref)
    qsq = qt * qt
    kmax = kmax_ref[...]
    shifts = (jnp.sqrt(jnp.sum(jnp.where(in1, qsq, 0.0), axis=0, keepdims=True)) * kmax[:, 0:1],
              jnp.sqrt(jnp.sum(jnp.where(in1, 0.0, qsq), axis=0, keepdims=True))
              * kmax[:, QK:QK + 1])
    qaug = tuple(jnp.concatenate([qm, jnp.where(row == 0, -sh, 0.0)], axis=0).astype(bf)
                 for qm, sh in zip(qms, shifts))
    shift_max = jnp.maximum(jnp.max(shifts[0]), jnp.max(shifts[1]))
    lam, lam_init = _lam_values(lam_ref)

    def key_rows(c):
        if isinstance(c, int):
            return k_ref[c * TM:(c + 1) * TM, :]
        return k_ref[pl.ds(pl.multiple_of(c * TM, TM), TM), :]

    @pl.when(shift_max <= SHIFT_SAFE)
    def _():
        ones_blk = jnp.ones((TM, V7X_LANES), bf)

        def body(c, stats):
            kaug = jnp.concatenate([key_rows(c), ones_blk], axis=1)
            vt = vt_ref[c]
            sts = [jnp.dot(kaug, qa, preferred_element_type=f32) for qa in qaug]
            ps = [jnp.exp2(st) for st in sts]
            out = []
            for p, (ls, acc) in zip(ps, stats):
                ls = ls + jnp.sum(p.reshape(TM // 8, 8, TQ), axis=0)
                acc = acc + jnp.dot(vt, p.astype(bf), preferred_element_type=f32)
                out.append((ls, acc))
            return tuple(out)

        init = tuple((jnp.zeros((8, TQ), f32), jnp.zeros((DV, TQ), f32)) for _ in range(2))
        assert (nchunks - 1) % UNROLL == 0

        def step(i, stats):
            for u in range(UNROLL):
                stats = body(1 + UNROLL * i + u, stats)
            return stats

        (l1, a1), (l2, a2) = lax.fori_loop(0, (nchunks - 1) // UNROLL, step, body(0, init))
        _finish_heads(a1 / jnp.sum(l1, axis=0, keepdims=True),
                      a2 / jnp.sum(l2, axis=0, keepdims=True), lam, lam_init, g_ref, ga_ref, o_ref)

    @pl.when(jnp.logical_not(shift_max <= SHIFT_SAFE))
    def _():
        def body(c, carry):
            k = key_rows(c)
            vt = vt_ref[c]
            return tuple(_online_step(k, vt, qa[0:2 * QK], *st) for qa, st in zip(qaug, carry))

        (_, l1, a1), (_, l2, a2) = lax.fori_loop(0, nchunks, body,
                                                 (_online_init(TQ), _online_init(TQ)))
        _finish_heads(a1 / l1, a2 / l2, lam, lam_init, g_ref, ga_ref, o_ref)


def _attn_call(lamv, z, qt, vt3, g_sub, gmat, with_ctx):
    nchunks = vt3.shape[0]
    lat = functools.partial(_attn_lat_kernel, nchunks=nchunks)
    ag = pl.pallas_call(
        lat,
        grid=(HEADS, SEQ // TQ),
        in_specs=[
            pl.BlockSpec((5, QK), lambda h, i: (0, 0)),
            pl.BlockSpec((2 * QK, TQ), lambda h, i: (h, i)),
            pl.BlockSpec((T, 2 * QK), lambda h, i: (0, ZC_K // DV + h)),
            pl.BlockSpec((nchunks, DV, TM), lambda h, i: (0, h, 0)),
            pl.BlockSpec((TQ, DV), lambda h, i: (i, ZC_GA // DV + h)),
            pl.BlockSpec((DV, 1), lambda h, i: (0, 0)),
            pl.BlockSpec((V7X_LANES, V7X_LANES), lambda h, i: (0, 0)),
        ],
        out_specs=pl.BlockSpec((TQ, DV), lambda h, i: (i, h)),
        out_shape=jax.ShapeDtypeStruct((T, AW), jnp.bfloat16),
        scratch_shapes=[pltpu.VMEM((1, 2 * QK), jnp.float32)],
        compiler_params=pltpu.CompilerParams(
            dimension_semantics=("arbitrary", "arbitrary"),
            vmem_limit_bytes=56 * 1024 * 1024),
        name="diff_attn",
    )(lamv, qt, z, vt3, z, g_sub, gmat)
    if not with_ctx:
        return ag
    cb = SEQ // CTX
    assert (SEQ % TM) % CTX == 0 and SEQ % TM + CTX == TM
    return pl.pallas_call(
        _attn_ctx_kernel,
        grid=(HEADS,),
        in_specs=[
            pl.BlockSpec((5, QK), lambda h: (0, 0)),
            pl.BlockSpec((2 * QK, CTX), lambda h: (h, cb)),
            pl.BlockSpec((CTX, 2 * QK), lambda h: (cb, ZC_K // DV + h)),
            pl.BlockSpec((1, DV, CTX), lambda h: (SEQ // TM, h, (SEQ % TM) // CTX)),
            pl.BlockSpec((CTX, DV), lambda h: (cb, ZC_GA // DV + h)),
            pl.BlockSpec((DV, 1), lambda h: (0, 0)),
            pl.BlockSpec(memory_space=pl.ANY),
        ],
        out_specs=pl.BlockSpec((CTX, DV), lambda h: (cb, h)),
        out_shape=jax.ShapeDtypeStruct((T, AW), jnp.bfloat16),
        input_output_aliases={6: 0},
        compiler_params=pltpu.CompilerParams(dimension_semantics=("arbitrary",)),
        name="ctx_attn",
    )(lamv, qt, z, vt3, z, g_sub, ag)


def _merge_kernel(h_ref, mod_ref, ag_ref, u_ref, v_ref, gb_ref, ma_ref, mb_ref, gv_ref, ws_ref,
                  bs_ref, wpa_ref, wpb_ref, wo_ref, *rest, has_next):
    if has_next:
        modn_ref, gn_ref, o_ref, hn_ref, bo_ref = rest
    else:
        o_ref, bo_ref = rest
    i = pl.program_id(0)
    f32 = jnp.float32
    n_blocks = (TB // CHUNK) * GROUPS
    slab = D * GMLP_BLOCKS_PER_YA_SLAB // n_blocks
    ya_slabs = []
    for ck in range(TB // CHUNK):
        r0 = ck * CHUNK
        for g in range(GROUPS):
            idx = ck * GROUPS + g
            if idx % GMLP_BLOCKS_PER_YA_SLAB == 0:
                n0 = (idx // GMLP_BLOCKS_PER_YA_SLAB) * slab
                ya_slabs.append(jnp.dot(ag_ref[...], wpa_ref[:, n0:n0 + slab],
                                        preferred_element_type=f32))
            c0 = g * GDIM
            blk = (slice(r0, r0 + CHUNK), slice(c0, c0 + GDIM))
            v = _gelu_tanh(v_ref[blk].astype(f32))
            vn = v * lax.rsqrt(jnp.mean(v * v, axis=-1, keepdims=True) + EPS) * gv_ref[:, c0:c0 + GDIM]
            s = jnp.dot(ws_ref[g], vn.astype(jnp.bfloat16), preferred_element_type=f32) + bs_ref[g]
            gb = gb_ref[blk].astype(f32)
            b = _gelu_tanh(u_ref[blk].astype(f32)) * s * (gb * _sigmoid(gb))
            bo_ref[blk] = b.astype(bo_ref.dtype)
    ya = jnp.concatenate(ya_slabs, axis=1)
    yb = jnp.dot(bo_ref[...], wpb_ref[...], preferred_element_type=f32)
    mix =_sigmoid(ma_ref[...].astype(f32)) * ya + _sigmoid(mb_ref[...].astype(f32)) * yb
    out = jnp.dot(mix.astype(jnp.bfloat16), wo_ref[...], preferred_element_type=jnp.float32)
    is_ctx = i == SEQ // TB
    gate = jnp.where(is_ctx, mod_ref[1:2, 2 * D:3 * D], mod_ref[0:1, 2 * D:3 * D])
    h_new = h_ref[...] + gate * out
    o_ref[...] = h_new
    if has_next:
        shift = jnp.where(is_ctx, modn_ref[1:2, 0:D], modn_ref[0:1, 0:D])
        scale = jnp.where(is_ctx, modn_ref[1:2, D:2 * D], modn_ref[0:1, D:2 * D])
        y = h_new * lax.rsqrt(jnp.mean(h_new * h_new, axis=-1, keepdims=True) + EPS) * gn_ref[...]
        hn_ref[...] = (y * (1.0 + scale) + shift).astype(hn_ref.dtype)


def _merge_call(h, mod_l, ag, z, gv, ws, bs, wpa, wpb, wo, nxt):
    const = dict(pipeline_mode=pl.Buffered(1))
    rows = SEQ if nxt is None else T
    row_blk = pl.BlockSpec((TB, D), lambda i: (i, 0))
    nxt_specs = [] if nxt is None else [pl.BlockSpec((2, 3 * D), lambda i: (0, 0)),
                                        pl.BlockSpec((1, D), lambda i: (0, 0))]
    h_out = jax.ShapeDtypeStruct((rows, D), jnp.float32)
    return pl.pallas_call(
        functools.partial(_merge_kernel, has_next=nxt is not None),
        grid=(rows // TB,),
        in_specs=[
            pl.BlockSpec((TB, D), lambda i: (i, 0)),
            pl.BlockSpec((2, 3 * D), lambda i: (0, 0)),
            pl.BlockSpec((TB, AW), lambda i: (i, 0)),
            pl.BlockSpec((TB, BW), lambda i: (i, ZC_U // BW)),
            pl.BlockSpec((TB, BW), lambda i: (i, ZC_VN // BW)),
            pl.BlockSpec((TB, BW), lambda i: (i, ZC_GB // BW)),
            pl.BlockSpec((TB, D), lambda i: (i, ZC_MG // D)),
            pl.BlockSpec((TB, D), lambda i: (i, ZC_MG // D + 1)),
            pl.BlockSpec((1, BW), lambda i: (0, 0)),
            pl.BlockSpec((GROUPS, CHUNK, CHUNK), lambda i: (0, 0, 0)),
            pl.BlockSpec((GROUPS, CHUNK, GDIM), lambda i: (0, 0, 0)),
            pl.BlockSpec((AW, D), lambda i: (0, 0), **const),
            pl.BlockSpec((BW, D), lambda i: (0, 0), **const),
            pl.BlockSpec((D, D), lambda i: (0, 0), **const),
        ] + nxt_specs,
        out_specs=row_blk if nxt is None else [row_blk, row_blk],
        out_shape=h_out if nxt is None else [h_out, jax.ShapeDtypeStruct((rows, D), jnp.bfloat16)],
        scratch_shapes=[pltpu.VMEM((TB, BW), jnp.bfloat16)],
        input_output_aliases={} if nxt is None else {0: 0},
        compiler_params=pltpu.CompilerParams(
            dimension_semantics=("arbitrary",),
            vmem_limit_bytes=48 * 1024 * 1024),
        name="gmlp_merge",
    )(h, mod_l, ag, z, z, z, z, z, gv, ws, bs, wpa, wpb, wo, *(() if nxt is None else nxt))


def _rope_tables(seq):
    rows = seq // GRID_W
    row = jnp.repeat(jnp.arange(rows, dtype=jnp.int32), GRID_W).astype(jnp.float32)
    col = jnp.tile(jnp.arange(GRID_W, dtype=jnp.int32), rows).astype(jnp.float32)
    inv = ROPE_BASE ** (-jnp.arange(ROPE_F, dtype=jnp.float32) / ROPE_F)
    ang = jnp.concatenate([row[:, None] * inv, col[:, None] * inv], axis=1)
    cos = jnp.concatenate([jnp.cos(ang), jnp.ones((CTX, 2 * ROPE_F), jnp.float32)], axis=0)
    sin = jnp.concatenate([jnp.sin(ang), jnp.zeros((CTX, 2 * ROPE_F), jnp.float32)], axis=0)
    cr, cc = cos[:, :ROPE_F], cos[:, ROPE_F:]
    sr, sc = sin[:, :ROPE_F], sin[:, ROPE_F:]
    cq = jnp.tile(jnp.concatenate([cr, cr, cc, cc], axis=1), (1, 2))
    sq = jnp.tile(jnp.concatenate([-sr, sr, -sc, sc], axis=1), (1, 2))
    return cq, sq, cos.T, sin.T


def kernel(x, c, ctx, c_ctx, w_ada, b_ada, norm_g, w_in, q_norm_g, k_norm_g, lam_q1, lam_k1,
           lam_q2, lam_k2, subln_g, v_norm_g, w_spatial, b_spatial, w_proj_a, w_proj_b, w_out):
    assert x.shape == (1, SEQ, D) and ctx.shape == (1, CTX, D)
    assert T % TM == 0 and SEQ % TQ == 0 and T % TP == 0 and TB == CTX

    bf = jnp.bfloat16
    cq, sq, ct, st = _rope_tables(SEQ)
    mod = _mod_call(jnp.stack([c[0], c_ctx], axis=1), w_ada, b_ada)
    gmat = (jnp.arange(V7X_LANES)[:, None] // QK == jnp.arange(V7X_LANES)[None, :] // QK).astype(bf)

    nq = HEADS * 2 * QK
    w_qt = jnp.swapaxes(w_in[:, :, :nq], 1, 2).astype(bf)
    w_vt = jnp.swapaxes(w_in[:, :, 2 * nq:2 * nq + AW], 1, 2).astype(bf)
    ws = w_spatial.astype(bf)
    bs = jnp.broadcast_to(b_spatial[..., None], b_spatial.shape + (GDIM,))
    wpa, wpb, wo = w_proj_a.astype(bf), w_proj_b.astype(bf), w_out.astype(bf)
    gk = jnp.tile(k_norm_g, (1, V7X_LANES // QK))
    lam_init = jnp.asarray([0.8 - 0.6 * math.exp(-0.3 * l) for l in range(DEPTH)], jnp.float32)
    lamv = jnp.stack([lam_q1, lam_k1, lam_q2, lam_k2,
                      jnp.broadcast_to(lam_init[:, None], lam_q1.shape)], axis=1)

    h = jnp.concatenate([x[0], ctx[0]], axis=0)
    hn = _prenorm_call(h, mod[0], norm_g[0][None, :])
    for l in range(DEPTH):
        z, qt, vt3 = _inproj_call(hn, w_in, l, w_qt[l], w_vt[l], q_norm_g[l][:, None],
                                  gk[l][None, :], cq, sq, ct, st, gmat)
        last = l == DEPTH - 1
        ag = _attn_call(lamv[l], z, qt, vt3, subln_g[l][:, None], gmat, with_ctx=not last)
        nxt = None if last else (mod[l + 1], norm_g[l + 1][None, :])
        out = _merge_call(h, mod[l], ag, z, v_norm_g[l].reshape(1, BW), ws[l], bs[l], wpa[l],
                          wpb[l], wo[l], nxt)
        h, hn = (out, None) if last else out
    return h[None]
```

```python
import functools
import math

import jax
import jax.numpy as jnp
from jax import lax
from jax.experimental import pallas as pl
from jax.experimental.pallas import tpu as pltpu

D = 2048
DEPTH = 4
GRID_W = 64
SEQ = 4 * GRID_W * GRID_W
CTX = 256
T = SEQ + CTX
HEADS = 8
QK = 64
DV = 128
AW = HEADS * DV
ROPE_F = 16
ROPE_BASE = 10000.0
GROUPS = 8
CHUNK = 128
GDIM = 128
BW = GROUPS * GDIM
EPS = 1e-6
LOG2E = 1.4426950408889634

V7X_LANES = 128
V7X_VMEM_BYTES = 64 * 1024 * 1024

ZC_K, ZC_GA, ZC_MG, ZC_U, ZC_VN, ZC_GB = 0, 1024, 2048, 6144, 7168, 8192
ZW = ZC_GB + 1024

SHIFT_SAFE = 50.0
BOUND_SLACK = 1.0 + 2.0 ** -7

TM = 1280
TN = 512
TQ = 1024
UNROLL = 4
TB = 256
GMLP_BLOCKS_PER_YA_SLAB = 2


def _sigmoid(x):
    return 1.0 / (1.0 + jnp.exp(-x))


def _gelu_tanh(x):
    return 0.5 * x * (1.0 + jnp.tanh(math.sqrt(2.0 / math.pi) * (x + 0.044715 * (x * x * x))))


def _mod_kernel(c_ref, w_ref, b_ref, o_ref):
    c = c_ref[...]
    s = c * _sigmoid(c)
    w = w_ref[0]
    b = b_ref[0]
    r0 = jnp.sum(w * s[:, 0:1], axis=0, keepdims=True) + b
    r1 = jnp.sum(w * s[:, 1:2], axis=0, keepdims=True) + b
    o_ref[0] = jnp.concatenate([r0, r1], axis=0)


def _mod_call(c2, w_ada, b_ada):
    tn = 512
    n = w_ada.shape[-1]
    return pl.pallas_call(
        _mod_kernel,
        grid=(DEPTH, n // tn),
        in_specs=[
            pl.BlockSpec((D, 2), lambda l, j: (0, 0)),
            pl.BlockSpec((1, D, tn), lambda l, j: (l, 0, j)),
            pl.BlockSpec((1, 1, tn), lambda l, j: (l, 0, j)),
        ],
        out_specs=pl.BlockSpec((1, 2, tn), lambda l, j: (l, 0, j)),
        out_shape=jax.ShapeDtypeStruct((DEPTH, 2, n), jnp.float32),
        compiler_params=pltpu.CompilerParams(dimension_semantics=("arbitrary", "arbitrary")),
        name="adaln_mod",
    )(c2, w_ada, b_ada.reshape(DEPTH, 1, n))


def _stream_rows(x_ref, c_ref, is_ctx):
    return jnp.where(is_ctx, c_ref[0], x_ref[0])


def _stream_specs():
    return [pl.BlockSpec((1, TB, D), lambda i: (0, jnp.minimum(i, SEQ // TB - 1), 0)),
            pl.BlockSpec((1, CTX, D), lambda i: (0, 0, 0))]


def _adaln_norm(h, mod_ref, g_ref, is_ctx):
    shift = jnp.where(is_ctx, mod_ref[1:2, 0:D], mod_ref[0:1, 0:D])
    scale = jnp.where(is_ctx, mod_ref[1:2, D:2 * D], mod_ref[0:1, D:2 * D])
    y = h * lax.rsqrt(jnp.mean(h * h, axis=-1, keepdims=True) + EPS) * g_ref[...]
    return (y * (1.0 + scale) + shift).astype(jnp.bfloat16)


def _prenorm_kernel(x_ref, c_ref, mod_ref, g_ref, o_ref):
    is_ctx = pl.program_id(0) == SEQ // TB
    o_ref[...] = _adaln_norm(_stream_rows(x_ref, c_ref, is_ctx), mod_ref, g_ref, is_ctx)


def _prenorm_call(x, ctx, mod_l, g):
    return pl.pallas_call(
        _prenorm_kernel,
        grid=(T // TB,),
        in_specs=_stream_specs() + [
            pl.BlockSpec((2, 3 * D), lambda i: (0, 0)),
            pl.BlockSpec((1, D), lambda i: (0, 0)),
        ],
        out_specs=pl.BlockSpec((TB, D), lambda i: (i, 0)),
        out_shape=jax.ShapeDtypeStruct((T, D), jnp.bfloat16),
        compiler_params=pltpu.CompilerParams(dimension_semantics=("arbitrary",)),
        name="prenorm",
    )(x, ctx, mod_l, g)


J_K0, J_VT0, J_VT1 = 2, 4, 6
NJ = J_VT1 + (ZW - ZC_GA) // TN
W_K0 = 2 * HEADS * QK // TN
W_U0 = J_VT1 + AW // TN
W_MG0 = W_U0 + 3 * BW // TN


def _inproj_kernel(x_ref, w_ref, wqt_ref, wvt_ref, gq_ref, gk_ref, cq_ref, sq_ref, ct_ref, st_ref,
                   gmat_ref, z_ref, qt_ref, vt_ref, acc_ref):
    j = pl.program_id(1)

    def project():
        return jnp.dot(x_ref[...], w_ref[0].astype(jnp.bfloat16),
                       preferred_element_type=jnp.float32)

    def project_t(wt_ref):
        return lax.dot_general(wt_ref[0], x_ref[...], (((1,), (1,)), ((), ())),
                               preferred_element_type=jnp.float32)

    @pl.when(j < J_K0)
    def _():
        zt = project_t(wqt_ref)
        gq = gq_ref[...]
        for hm in range(TN // QK):
            blk = zt[hm * QK:(hm + 1) * QK]
            inv = lax.rsqrt(jnp.sum(blk * blk, axis=0, keepdims=True) * (1.0 / QK) + EPS)
            qn = blk * inv * gq
            for ax in range(2):
                x1 = qn[ax * 2 * ROPE_F:ax * 2 * ROPE_F + ROPE_F]
                x2 = qn[ax * 2 * ROPE_F + ROPE_F:(ax + 1) * 2 * ROPE_F]
                cs = ct_ref[ax * ROPE_F:(ax + 1) * ROPE_F]
                sn = st_ref[ax * ROPE_F:(ax + 1) * ROPE_F]
                base = hm * QK + ax * 2 * ROPE_F
                qt_ref[base:base + ROPE_F, :] = (
                    (x1 * cs - x2 * sn) * (QK ** -0.5 * LOG2E)).astype(qt_ref.dtype)
                qt_ref[base + ROPE_F:base + 2 * ROPE_F, :] = (
                    (x2 * cs + x1 * sn) * (QK ** -0.5 * LOG2E)).astype(qt_ref.dtype)

    @pl.when((j >= J_K0) & (j < J_VT0))
    def _():
        acc_ref[...] = project()
        for g in range(TN // V7X_LANES):
            z = acc_ref[:, g * V7X_LANES:(g + 1) * V7X_LANES]
            ss = jnp.dot((z * z).astype(jnp.bfloat16), gmat_ref[...],
                         preferred_element_type=jnp.float32)
            kn = z * lax.rsqrt(ss * (1.0 / QK) + EPS) * gk_ref[...]
            lane = lax.broadcasted_iota(jnp.int32, kn.shape, 1)
            partner = jnp.where((lane % (2 * ROPE_F)) < ROPE_F,
                                pltpu.roll(kn, V7X_LANES - ROPE_F, 1),
                                pltpu.roll(kn, ROPE_F, 1))
            z_ref[:, g * V7X_LANES:(g + 1) * V7X_LANES] = (
                kn * cq_ref[...] + partner * sq_ref[...]).astype(z_ref.dtype)

    @pl.when((j >= J_VT0) & (j < J_VT1))
    def _():
        vt_ref[0] = project_t(wvt_ref).astype(vt_ref.dtype)

    @pl.when(j >= J_VT1)
    def _():
        z_ref[...] = project().astype(z_ref.dtype)


def _inproj_call(hn, w_in, layer, wqt, wvt, gq, gk, cq, sq, ct, st, gmat):
    t = hn.shape[0]

    def wtile(i, j):
        return jnp.where(j < J_K0, W_K0, jnp.where((j >= J_VT0) & (j < J_VT1), W_K0 + 1, j))

    def ztile(i, j):
        head = jnp.where(j < J_VT1, jnp.clip(j - J_K0, 0, 1), j - J_VT1 + ZC_GA // TN)
        return jnp.where(j < W_U0, head,
                         jnp.where(j < W_MG0, j - W_U0 + ZC_U // TN, j - W_MG0 + ZC_MG // TN))

    return pl.pallas_call(
        _inproj_kernel,
        grid=(t // TM, NJ),
        in_specs=[
            pl.BlockSpec((TM, D), lambda i, j: (i, 0)),
            pl.BlockSpec((1, D, TN), lambda i, j: (layer, 0, wtile(i, j))),
            pl.BlockSpec((1, TN, D), lambda i, j: (layer, jnp.clip(j, 0, 1), 0)),
            pl.BlockSpec((1, TN, D), lambda i, j: (layer, jnp.clip(j - J_VT0, 0, 1), 0)),
            pl.BlockSpec((QK, 1), lambda i, j: (0, 0)),
            pl.BlockSpec((1, V7X_LANES), lambda i, j: (0, 0)),
            pl.BlockSpec((TM, V7X_LANES), lambda i, j: (i, 0)),
            pl.BlockSpec((TM, V7X_LANES), lambda i, j: (i, 0)),
            pl.BlockSpec((2 * ROPE_F, TM), lambda i, j: (0, i)),
            pl.BlockSpec((2 * ROPE_F, TM), lambda i, j: (0, i)),
            pl.BlockSpec((V7X_LANES, V7X_LANES), lambda i, j: (0, 0)),
        ],
        out_specs=[
            pl.BlockSpec((TM, TN), lambda i, j: (i, ztile(i, j))),
            pl.BlockSpec((TN, TM), lambda i, j: (jnp.clip(j, 0, 1), i)),
            pl.BlockSpec((1, TN, TM), lambda i, j: (i, jnp.clip(j - J_VT0, 0, 1), 0)),
        ],
        out_shape=[
            jax.ShapeDtypeStruct((t, ZW), jnp.bfloat16),
            jax.ShapeDtypeStruct((AW, t), jnp.bfloat16),
            jax.ShapeDtypeStruct((t // TM, AW, TM), jnp.bfloat16),
        ],
        scratch_shapes=[pltpu.VMEM((TM, TN), jnp.float32)],
        compiler_params=pltpu.CompilerParams(
            dimension_semantics=("arbitrary", "arbitrary"),
            vmem_limit_bytes=56 * 1024 * 1024),
        name="inproj",
    )(hn, w_in, wqt, wvt, gq, gk, cq, sq, ct, st, gmat)


def _lam_values(lam_ref):
    lv = lam_ref[...]
    lam_init = lv[4:5, 0:1]
    lam = (jnp.exp(jnp.sum(lv[0:1] * lv[1:2], axis=-1, keepdims=True))
           - jnp.exp(jnp.sum(lv[2:3] * lv[3:4], axis=-1, keepdims=True)) + lam_init)
    return lam, lam_init


def _finish_heads(o1, o2, lam, lam_init, g_ref, ga_ref, o_ref):
    d = o1 - lam * o2
    dn = d * lax.rsqrt(jnp.mean(d * d, axis=0, keepdims=True) + EPS)
    dn = dn * g_ref[...] * (1.0 - lam_init)
    ga = ga_ref[...].astype(jnp.float32)
    o_ref[...] = (dn.T * (ga * _sigmoid(ga))).astype(o_ref.dtype)


def _split_maps(qt_ref):
    qt = qt_ref[...].astype(jnp.float32)
    row = lax.broadcasted_iota(jnp.int32, qt.shape, 0)
    in1 = row < QK
    return qt, row, in1, (jnp.where(in1, qt, 0.0), jnp.where(in1, 0.0, qt))


def _online_step(k, vt, qt, m, l, acc):
    st = jnp.dot(k, qt, preferred_element_type=jnp.float32)
    m_new = jnp.maximum(m, jnp.max(st, axis=0, keepdims=True))
    alpha = jnp.exp2(m - m_new)
    p = jnp.exp2(st - m_new)
    l_new = alpha * l + jnp.sum(p, axis=0, keepdims=True)
    acc_new = alpha * acc + jnp.dot(vt, p.astype(jnp.bfloat16), preferred_element_type=jnp.float32)
    return m_new, l_new, acc_new


def _online_init(tq):
    return (jnp.full((1, tq), -jnp.inf, jnp.float32), jnp.zeros((1, tq), jnp.float32),
            jnp.zeros((DV, tq), jnp.float32))


def _attn_ctx_kernel(lam_ref, qt_ref, k_ref, vt_ref, ga_ref, g_ref, ag_ref, o_ref):
    del ag_ref
    _, _, _, qms = _split_maps(qt_ref)
    k = k_ref[...]
    vt = vt_ref[0]
    outs = []
    for qm in qms:
        _, l, acc = _online_step(k, vt, qm.astype(jnp.bfloat16), *_online_init(CTX))
        outs.append(acc / l)
    _finish_heads(outs[0], outs[1], *_lam_values(lam_ref), g_ref, ga_ref, o_ref)


def _attn_lat_kernel(lam_ref, qt_ref, k_ref, vt_ref, ga_ref, g_ref, gmat_ref, o_ref, kmax_ref,
                     *, nchunks):
    i = pl.program_id(1)
    f32, bf = jnp.float32, jnp.bfloat16

    @pl.when(i == 0)
    def _():
        def kbody(c, mx):
            kk = k_ref[pl.ds(pl.multiple_of(c * TM, TM), TM), :].astype(f32)
            nrm = jnp.dot((kk * kk).astype(bf), gmat_ref[...], preferred_element_type=f32)
            return jnp.maximum(mx, jnp.max(nrm.reshape(TM // 8, 8, 2 * QK), axis=0))
        mx = lax.fori_loop(0, nchunks, kbody, jnp.zeros((8, 2 * QK), f32))
        kmax_ref[...] = jnp.sqrt(jnp.max(mx, axis=0, keepdims=True)) * BOUND_SLACK

    qt, row, in1, qms = _split_maps(qt_ref)
    qsq = qt * qt
    kmax = kmax_ref[...]
    shifts = (jnp.sqrt(jnp.sum(jnp.where(in1, qsq, 0.0), axis=0, keepdims=True)) * kmax[:, 0:1],
              jnp.sqrt(jnp.sum(jnp.where(in1, 0.0, qsq), axis=0, keepdims=True))
              * kmax[:, QK:QK + 1])
    qaug = tuple(jnp.concatenate([qm, jnp.where(row == 0, -sh, 0.0)], axis=0).astype(bf)
                 for qm, sh in zip(qms, shifts))
    shift_max = jnp.maximum(jnp.max(shifts[0]), jnp.max(shifts[1]))
    lam, lam_init = _lam_values(lam_ref)

    def key_rows(c):
        if isinstance(c, int):
            return k_ref[c * TM:(c + 1) * TM, :]
        return k_ref[pl.ds(pl.multiple_of(c * TM, TM), TM), :]

    @pl.when(shift_max <= SHIFT_SAFE)
    def _():
        ones_blk = jnp.ones((TM, V7X_LANES), bf)

        def body(c, stats):
            kaug = jnp.concatenate([key_rows(c), ones_blk], axis=1)
            vt = vt_ref[c]
            sts = [jnp.dot(kaug, qa, preferred_element_type=f32) for qa in qaug]
            ps = [jnp.exp2(st) for st in sts]
            out = []
            for p, (ls, acc) in zip(ps, stats):
                ls = ls + jnp.sum(p.reshape(TM // 8, 8, TQ), axis=0)
                acc = acc + jnp.dot(vt, p.astype(bf), preferred_element_type=f32)
                out.append((ls, acc))
            return tuple(out)

        init = tuple((jnp.zeros((8, TQ), f32), jnp.zeros((DV, TQ), f32)) for _ in range(2))
        assert (nchunks - 1) % UNROLL == 0

        def step(i, stats):
            for u in range(UNROLL):
                stats = body(1 + UNROLL * i + u, stats)
            return stats

        (l1, a1), (l2, a2) = lax.fori_loop(0, (nchunks - 1) // UNROLL, step, body(0, init))
        _finish_heads(a1 / jnp.sum(l1, axis=0, keepdims=True),
                      a2 / jnp.sum(l2, axis=0, keepdims=True), lam, lam_init, g_ref, ga_ref, o_ref)

    @pl.when(jnp.logical_not(shift_max <= SHIFT_SAFE))
    def _():
        def body(c, carry):
            k = key_rows(c)
            vt = vt_ref[c]
            return tuple(_online_step(k, vt, qa[0:2 * QK], *st) for qa, st in zip(qaug, carry))

        (_, l1, a1), (_, l2, a2) = lax.fori_loop(0, nchunks, body,
                                                 (_online_init(TQ), _online_init(TQ)))
        _finish_heads(a1 / l1, a2 / l2, lam, lam_init, g_ref, ga_ref, o_ref)


def _attn_call(lamv, z, qt, vt3, g_sub, gmat, with_ctx):
    nchunks = vt3.shape[0]
    lat = functools.partial(_attn_lat_kernel, nchunks=nchunks)
    ag = pl.pallas_call(
        lat,
        grid=(HEADS, SEQ // TQ),
        in_specs=[
            pl.BlockSpec((5, QK), lambda h, i: (0, 0)),
            pl.BlockSpec((2 * QK, TQ), lambda h, i: (h, i)),
            pl.BlockSpec((T, 2 * QK), lambda h, i: (0, ZC_K // DV + h)),
            pl.BlockSpec((nchunks, DV, TM), lambda h, i: (0, h, 0)),
            pl.BlockSpec((TQ, DV), lambda h, i: (i, ZC_GA // DV + h)),
            pl.BlockSpec((DV, 1), lambda h, i: (0, 0)),
            pl.BlockSpec((V7X_LANES, V7X_LANES), lambda h, i: (0, 0)),
        ],
        out_specs=pl.BlockSpec((TQ, DV), lambda h, i: (i, h)),
        out_shape=jax.ShapeDtypeStruct((T, AW), jnp.bfloat16),
        scratch_shapes=[pltpu.VMEM((1, 2 * QK), jnp.float32)],
        compiler_params=pltpu.CompilerParams(
            dimension_semantics=("arbitrary", "arbitrary"),
            vmem_limit_bytes=56 * 1024 * 1024),
        name="diff_attn",
    )(lamv, qt, z, vt3, z, g_sub, gmat)
    if not with_ctx:
        return ag
    cb = SEQ // CTX
    assert (SEQ % TM) % CTX == 0 and SEQ % TM + CTX == TM
    return pl.pallas_call(
        _attn_ctx_kernel,
        grid=(HEADS,),
        in_specs=[
            pl.BlockSpec((5, QK), lambda h: (0, 0)),
            pl.BlockSpec((2 * QK, CTX), lambda h: (h, cb)),
            pl.BlockSpec((CTX, 2 * QK), lambda h: (cb, ZC_K // DV + h)),
            pl.BlockSpec((1, DV, CTX), lambda h: (SEQ // TM, h, (SEQ % TM) // CTX)),
            pl.BlockSpec((CTX, DV), lambda h: (cb, ZC_GA // DV + h)),
            pl.BlockSpec((DV, 1), lambda h: (0, 0)),
            pl.BlockSpec(memory_space=pl.ANY),
        ],
        out_specs=pl.BlockSpec((CTX, DV), lambda h: (cb, h)),
        out_shape=jax.ShapeDtypeStruct((T, AW), jnp.bfloat16),
        input_output_aliases={6: 0},
        compiler_params=pltpu.CompilerParams(dimension_semantics=("arbitrary",)),
        name="ctx_attn",
    )(lamv, qt, z, vt3, z, g_sub, ag)


def _merge_kernel(*refs, first, has_next):
    n_res = 2 if first else 1
    (mod_ref, ag_ref, u_ref, v_ref, gb_ref, ma_ref, mb_ref, gv_ref, ws_ref, bs_ref, wpa_ref,
     wpb_ref, wo_ref) = refs[n_res:n_res + 13]
    rest = refs[n_res + 13:]
    if has_next:
        modn_ref, gn_ref, o_ref, hn_ref, bo_ref = rest
    else:
        o_ref, bo_ref = rest
    i = pl.program_id(0)
    is_ctx = i == SEQ // TB
    f32 = jnp.float32
    n_blocks = (TB // CHUNK) * GROUPS
    slab = D * GMLP_BLOCKS_PER_YA_SLAB // n_blocks
    ya_slabs = []
    for ck in range(TB // CHUNK):
        r0 = ck * CHUNK
        for g in range(GROUPS):
            idx = ck * GROUPS + g
            if idx % GMLP_BLOCKS_PER_YA_SLAB == 0:
                n0 = (idx // GMLP_BLOCKS_PER_YA_SLAB) * slab
                ya_slabs.append(jnp.dot(ag_ref[...], wpa_ref[0, :, n0:n0 + slab],
                                        preferred_element_type=f32))
            c0 = g * GDIM
            blk = (slice(r0, r0 + CHUNK), slice(c0, c0 + GDIM))
            v = _gelu_tanh(v_ref[blk].astype(f32))
            vn = v * lax.rsqrt(jnp.mean(v * v, axis=-1, keepdims=True) + EPS) * gv_ref[:, c0:c0 + GDIM]
            s = (jnp.dot(ws_ref[0, g], vn.astype(jnp.bfloat16), preferred_element_type=f32)
                 + bs_ref[0, g])
            gb = gb_ref[blk].astype(f32)
            b = _gelu_tanh(u_ref[blk].astype(f32)) * s * (gb * _sigmoid(gb))
            bo_ref[blk] = b.astype(bo_ref.dtype)
    ya = jnp.concatenate(ya_slabs, axis=1)
    yb = jnp.dot(bo_ref[...], wpb_ref[0], preferred_element_type=f32)
    mix =_sigmoid(ma_ref[...].astype(f32)) * ya + _sigmoid(mb_ref[...].astype(f32)) * yb
    out = jnp.dot(mix.astype(jnp.bfloat16), wo_ref[0], preferred_element_type=jnp.float32)
    gate = jnp.where(is_ctx, mod_ref[1:2, 2 * D:3 * D], mod_ref[0:1, 2 * D:3 * D])
    h = _stream_rows(refs[0], refs[1], is_ctx) if first else refs[0][...]
    h_new = h + gate * out
    o_ref[...] = h_new
    if has_next:
        hn_ref[...] = _adaln_norm(h_new, modn_ref, gn_ref, is_ctx)


def _merge_call(res, mod_l, ag, z, gv, ws, bs, wpa, wpb, wo, layer, nxt):
    first = len(res) == 2
    const = dict(pipeline_mode=pl.Buffered(1))
    rows = SEQ if nxt is None else T
    row_blk = pl.BlockSpec((TB, D), lambda i: (i, 0))
    res_specs = _stream_specs() if first else [row_blk]
    nxt_specs = [] if nxt is None else [pl.BlockSpec((2, 3 * D), lambda i: (0, 0)),
                                        pl.BlockSpec((1, D), lambda i: (0, 0))]
    h_out = jax.ShapeDtypeStruct((rows, D), jnp.float32)
    return pl.pallas_call(
        functools.partial(_merge_kernel, first=first, has_next=nxt is not None),
        grid=(rows // TB,),
        in_specs=res_specs + [
            pl.BlockSpec((2, 3 * D), lambda i: (0, 0)),
            pl.BlockSpec((TB, AW), lambda i: (i, 0)),
            pl.BlockSpec((TB, BW), lambda i: (i, ZC_U // BW)),
            pl.BlockSpec((TB, BW), lambda i: (i, ZC_VN // BW)),
            pl.BlockSpec((TB, BW), lambda i: (i, ZC_GB // BW)),
            pl.BlockSpec((TB, D), lambda i: (i, ZC_MG // D)),
            pl.BlockSpec((TB, D), lambda i: (i, ZC_MG // D + 1)),
            pl.BlockSpec((1, BW), lambda i: (0, 0)),
            pl.BlockSpec((1, GROUPS, CHUNK, CHUNK), lambda i: (layer, 0, 0, 0)),
            pl.BlockSpec((1, GROUPS, CHUNK, GDIM), lambda i: (layer, 0, 0, 0)),
            pl.BlockSpec((1, AW, D), lambda i: (layer, 0, 0), **const),
            pl.BlockSpec((1, BW, D), lambda i: (layer, 0, 0), **const),
            pl.BlockSpec((1, D, D), lambda i: (layer, 0, 0), **const),
        ] + nxt_specs,
        out_specs=row_blk if nxt is None else [row_blk, row_blk],
        out_shape=h_out if nxt is None else [h_out, jax.ShapeDtypeStruct((rows, D), jnp.bfloat16)],
        scratch_shapes=[pltpu.VMEM((TB, BW), jnp.bfloat16)],
        input_output_aliases={} if (first or nxt is None) else {0: 0},
        compiler_params=pltpu.CompilerParams(
            dimension_semantics=("arbitrary",),
            vmem_limit_bytes=48 * 1024 * 1024),
        name="gmlp_merge",
    )(*res, mod_l, ag, z, z, z, z, z, gv, ws, bs, wpa, wpb, wo, *(() if nxt is None else nxt))


def _rope_tables(seq):
    rows = seq // GRID_W
    row = jnp.repeat(jnp.arange(rows, dtype=jnp.int32), GRID_W).astype(jnp.float32)
    col = jnp.tile(jnp.arange(GRID_W, dtype=jnp.int32), rows).astype(jnp.float32)
    inv = ROPE_BASE ** (-jnp.arange(ROPE_F, dtype=jnp.float32) / ROPE_F)
    ang = jnp.concatenate([row[:, None] * inv, col[:, None] * inv], axis=1)
    cos = jnp.concatenate([jnp.cos(ang), jnp.ones((CTX, 2 * ROPE_F), jnp.float32)], axis=0)
    sin = jnp.concatenate([jnp.sin(ang), jnp.zeros((CTX, 2 * ROPE_F), jnp.float32)], axis=0)
    cr, cc = cos[:, :ROPE_F], cos[:, ROPE_F:]
    sr, sc = sin[:, :ROPE_F], sin[:, ROPE_F:]
    cq = jnp.tile(jnp.concatenate([cr, cr, cc, cc], axis=1), (1, 2))
    sq = jnp.tile(jnp.concatenate([-sr, sr, -sc, sc], axis=1), (1, 2))
    return cq, sq, cos.T, sin.T


def kernel(x, c, ctx, c_ctx, w_ada, b_ada, norm_g, w_in, q_norm_g, k_norm_g, lam_q1, lam_k1,
           lam_q2, lam_k2, subln_g, v_norm_g, w_spatial, b_spatial, w_proj_a, w_proj_b, w_out):
    assert x.shape == (1, SEQ, D) and ctx.shape == (1, CTX, D)
    assert T % TM == 0 and SEQ % TQ == 0 and TB == CTX

    bf = jnp.bfloat16
    cq, sq, ct, st = _rope_tables(SEQ)
    mod = _mod_call(jnp.stack([c[0], c_ctx], axis=1), w_ada, b_ada)
    gmat = (jnp.arange(V7X_LANES)[:, None] // QK == jnp.arange(V7X_LANES)[None, :] // QK).astype(bf)

    nq = HEADS * 2 * QK
    w_q, w_v = lax.optimization_barrier((w_in[:, :, :nq], w_in[:, :, 2 * nq:2 * nq + AW]))
    w_qt = jnp.swapaxes(w_q, 1, 2).astype(bf)
    w_vt = jnp.swapaxes(w_v, 1, 2).astype(bf)
    ws = w_spatial.astype(bf)
    bs = jnp.broadcast_to(b_spatial[..., None], b_spatial.shape + (GDIM,))
    wpa, wpb, wo = w_proj_a.astype(bf), w_proj_b.astype(bf), w_out.astype(bf)
    gk = jnp.tile(k_norm_g, (1, V7X_LANES // QK))
    lam_init = jnp.asarray([0.8 - 0.6 * math.exp(-0.3 * l) for l in range(DEPTH)], jnp.float32)
    lamv = jnp.stack([lam_q1, lam_k1, lam_q2, lam_k2,
                      jnp.broadcast_to(lam_init[:, None], lam_q1.shape)], axis=1)

    res = (x, ctx)
    hn = _prenorm_call(x, ctx, mod[0], norm_g[0][None, :])
    for l in range(DEPTH):
        z, qt, vt3 = _inproj_call(hn, w_in, l, w_qt, w_vt, q_norm_g[l][:, None],
                                  gk[l][None, :], cq, sq, ct, st, gmat)
        last = l == DEPTH - 1
        ag = _attn_call(lamv[l], z, qt, vt3, subln_g[l][:, None], gmat, with_ctx=not last)
        nxt = None if last else (mod[l + 1], norm_g[l + 1][None, :])
        out = _merge_call(res, mod[l], ag, z, v_norm_g[l].reshape(1, BW), ws, bs, wpa, wpb, wo, l,
                          nxt)
        h, hn = (out, None) if last else out
        res = (h,)
    return h[None]
```

```python
import functools
import math

import jax
import jax.numpy as jnp
from jax import lax
from jax.experimental import pallas as pl
from jax.experimental.pallas import tpu as pltpu

D = 2048
DEPTH = 4
GRID_W = 64
SEQ = 4 * GRID_W * GRID_W
CTX = 256
T = SEQ + CTX
HEADS = 8
QK = 64
DV = 128
AW = HEADS * DV
ROPE_F = 16
ROPE_BASE = 10000.0
GROUPS = 8
CHUNK = 128
GDIM = 128
BW = GROUPS * GDIM
EPS = 1e-6
LOG2E = 1.4426950408889634

V7X_LANES = 128
V7X_VMEM_BYTES = 64 * 1024 * 1024
VMEM_LIMIT_WIDE = V7X_VMEM_BYTES * 7 // 8
VMEM_LIMIT_MERGE = V7X_VMEM_BYTES * 3 // 4

ZC_K, ZC_GA, ZC_MG, ZC_U, ZC_VN, ZC_GB = 0, 1024, 2048, 6144, 7168, 8192
ZW = ZC_GB + 1024

SHIFT_SAFE = 50.0
BOUND_SLACK = 1.0 + 2.0 ** -7

TM = 1280
TN = 1024
TQ = 1024
TB = 256
GMLP_BLOCKS_PER_YA_SLAB = 2


def _sigmoid(x):
    return 1.0 / (1.0 + jnp.exp(-x))


def _gelu_tanh(x):
    return 0.5 * x * (1.0 + jnp.tanh(math.sqrt(2.0 / math.pi) * (x + 0.044715 * (x * x * x))))


def _mod_kernel(c_ref, w_ref, b_ref, o_ref):
    c = c_ref[...]
    s = c * _sigmoid(c)
    w = w_ref[0]
    b = b_ref[0]
    r0 = jnp.sum(w * s[:, 0:1], axis=0, keepdims=True) + b
    r1 = jnp.sum(w * s[:, 1:2], axis=0, keepdims=True) + b
    o_ref[0] = jnp.concatenate([r0, r1], axis=0)


def _mod_call(c2, w_ada, b_ada):
    tn = 512
    n = w_ada.shape[-1]
    return pl.pallas_call(
        _mod_kernel,
        grid=(DEPTH, n // tn),
        in_specs=[
            pl.BlockSpec((D, 2), lambda l, j: (0, 0)),
            pl.BlockSpec((1, D, tn), lambda l, j: (l, 0, j)),
            pl.BlockSpec((1, 1, tn), lambda l, j: (l, 0, j)),
        ],
        out_specs=pl.BlockSpec((1, 2, tn), lambda l, j: (l, 0, j)),
        out_shape=jax.ShapeDtypeStruct((DEPTH, 2, n), jnp.float32),
        compiler_params=pltpu.CompilerParams(dimension_semantics=("arbitrary", "arbitrary")),
        name="adaln_mod",
    )(c2, w_ada, b_ada.reshape(DEPTH, 1, n))


def _stream_rows(x_ref, c_ref, is_ctx):
    return jnp.where(is_ctx, c_ref[0], x_ref[0])


def _stream_specs():
    return [pl.BlockSpec((1, TB, D), lambda i: (0, jnp.minimum(i, SEQ // TB - 1), 0)),
            pl.BlockSpec((1, CTX, D), lambda i: (0, 0, 0))]


def _adaln_norm(h, mod_ref, g_ref, is_ctx):
    shift = jnp.where(is_ctx, mod_ref[1:2, 0:D], mod_ref[0:1, 0:D])
    scale = jnp.where(is_ctx, mod_ref[1:2, D:2 * D], mod_ref[0:1, D:2 * D])
    y = h * lax.rsqrt(jnp.mean(h * h, axis=-1, keepdims=True) + EPS) * g_ref[...]
    return (y * (1.0 + scale) + shift).astype(jnp.bfloat16)


def _prenorm_kernel(x_ref, c_ref, mod_ref, g_ref, o_ref):
    is_ctx = pl.program_id(0) == SEQ // TB
    o_ref[...] = _adaln_norm(_stream_rows(x_ref, c_ref, is_ctx), mod_ref, g_ref, is_ctx)


def _prenorm_call(x, ctx, mod_l, g):
    return pl.pallas_call(
        _prenorm_kernel,
        grid=(T // TB,),
        in_specs=_stream_specs() + [
            pl.BlockSpec((2, 3 * D), lambda i: (0, 0)),
            pl.BlockSpec((1, D), lambda i: (0, 0)),
        ],
        out_specs=pl.BlockSpec((TB, D), lambda i: (i, 0)),
        out_shape=jax.ShapeDtypeStruct((T, D), jnp.bfloat16),
        compiler_params=pltpu.CompilerParams(dimension_semantics=("arbitrary",)),
        name="prenorm",
    )(x, ctx, mod_l, g)


J_K0 = 2 * HEADS * QK // TN
J_VT0 = 2 * J_K0
J_VT1 = J_VT0 + AW // TN
NJ = J_VT1 + (ZW - ZC_GA) // TN
W_K0 = J_K0
W_U0 = J_VT1 + AW // TN
W_MG0 = W_U0 + 3 * BW // TN


def _inproj_kernel(x_ref, w_ref, wqt_ref, wvt_ref, gq_ref, gk_ref, cq_ref, sq_ref, ct_ref, st_ref,
                   gmat_ref, z_ref, qt_ref, vt_ref):
    j = pl.program_id(1)

    def project():
        return jnp.dot(x_ref[...], w_ref[0].astype(jnp.bfloat16),
                       preferred_element_type=jnp.float32)

    def project_t(wt_ref):
        return lax.dot_general(wt_ref[0], x_ref[...], (((1,), (1,)), ((), ())),
                               preferred_element_type=jnp.float32)

    @pl.when(j < J_K0)
    def _():
        zt = project_t(wqt_ref)
        gq = gq_ref[...]
        for hm in range(TN // QK):
            blk = zt[hm * QK:(hm + 1) * QK]
            inv = lax.rsqrt(jnp.sum(blk * blk, axis=0, keepdims=True) * (1.0 / QK) + EPS)
            qn = blk * inv * gq
            for ax in range(2):
                x1 = qn[ax * 2 * ROPE_F:ax * 2 * ROPE_F + ROPE_F]
                x2 = qn[ax * 2 * ROPE_F + ROPE_F:(ax + 1) * 2 * ROPE_F]
                cs = ct_ref[ax * ROPE_F:(ax + 1) * ROPE_F]
                sn = st_ref[ax * ROPE_F:(ax + 1) * ROPE_F]
                base = hm * QK + ax * 2 * ROPE_F
                qt_ref[base:base + ROPE_F, :] = (
                    (x1 * cs - x2 * sn) * (QK ** -0.5 * LOG2E)).astype(qt_ref.dtype)
                qt_ref[base + ROPE_F:base + 2 * ROPE_F, :] = (
                    (x2 * cs + x1 * sn) * (QK ** -0.5 * LOG2E)).astype(qt_ref.dtype)

    @pl.when((j >= J_K0) & (j < J_VT0))
    def _():
        acc = project()
        for g in range(TN // V7X_LANES):
            z = acc[:, g * V7X_LANES:(g + 1) * V7X_LANES]
            ss = jnp.dot((z * z).astype(jnp.bfloat16), gmat_ref[...],
                         preferred_element_type=jnp.float32)
            kn = z * lax.rsqrt(ss * (1.0 / QK) + EPS) * gk_ref[...]
            lane = lax.broadcasted_iota(jnp.int32, kn.shape, 1)
            partner = jnp.where((lane % (2 * ROPE_F)) < ROPE_F,
                                pltpu.roll(kn, V7X_LANES - ROPE_F, 1),
                                pltpu.roll(kn, ROPE_F, 1))
            z_ref[:, g * V7X_LANES:(g + 1) * V7X_LANES] = (
                kn * cq_ref[...] + partner * sq_ref[...]).astype(z_ref.dtype)

    @pl.when((j >= J_VT0) & (j < J_VT1))
    def _():
        vt_ref[0] = project_t(wvt_ref).astype(vt_ref.dtype)

    @pl.when(j >= J_VT1)
    def _():
        z_ref[...] = project().astype(z_ref.dtype)


def _inproj_call(hn, w_in, layer, wqt, wvt, gq, gk, cq, sq, ct, st, gmat):
    t = hn.shape[0]

    def wtile(i, j):
        last_k = W_K0 + (J_VT0 - J_K0) - 1
        return jnp.where(j < J_K0, W_K0, jnp.where((j >= J_VT0) & (j < J_VT1), last_k, j))

    def ztile(i, j):
        head = jnp.where(j < J_VT1, jnp.clip(j - J_K0, 0, J_VT0 - J_K0 - 1),
                         j - J_VT1 + ZC_GA // TN)
        return jnp.where(j < W_U0, head,
                         jnp.where(j < W_MG0, j - W_U0 + ZC_U // TN, j - W_MG0 + ZC_MG // TN))

    assert J_K0 == 1 and J_VT1 - J_VT0 == 1
    once = dict(pipeline_mode=pl.Buffered(1))
    return pl.pallas_call(
        _inproj_kernel,
        grid=(t // TM, NJ),
        in_specs=[
            pl.BlockSpec((TM, D), lambda i, j: (i, 0), **once),
            pl.BlockSpec((1, D, TN), lambda i, j: (layer, 0, wtile(i, j))),
            pl.BlockSpec((1, TN, D), lambda i, j: (layer, jnp.clip(j, 0, J_K0 - 1), 0), **once),
            pl.BlockSpec((1, TN, D),
                         lambda i, j: (layer, jnp.clip(j - J_VT0, 0, J_VT1 - J_VT0 - 1), 0), **once),
            pl.BlockSpec((QK, 1), lambda i, j: (0, 0)),
            pl.BlockSpec((1, V7X_LANES), lambda i, j: (0, 0)),
            pl.BlockSpec((TM, V7X_LANES), lambda i, j: (i, 0)),
            pl.BlockSpec((TM, V7X_LANES), lambda i, j: (i, 0)),
            pl.BlockSpec((2 * ROPE_F, TM), lambda i, j: (0, i)),
            pl.BlockSpec((2 * ROPE_F, TM), lambda i, j: (0, i)),
            pl.BlockSpec((V7X_LANES, V7X_LANES), lambda i, j: (0, 0)),
        ],
        out_specs=[
            pl.BlockSpec((TM, TN), lambda i, j: (i, ztile(i, j))),
            pl.BlockSpec((TN, TM), lambda i, j: (jnp.clip(j, 0, J_K0 - 1), i)),
            pl.BlockSpec((1, TN, TM),
                         lambda i, j: (i, jnp.clip(j - J_VT0, 0, J_VT1 - J_VT0 - 1), 0)),
        ],
        out_shape=[
            jax.ShapeDtypeStruct((t, ZW), jnp.bfloat16),
            jax.ShapeDtypeStruct((AW, t), jnp.bfloat16),
            jax.ShapeDtypeStruct((t // TM, AW, TM), jnp.bfloat16),
        ],
        compiler_params=pltpu.CompilerParams(
            dimension_semantics=("arbitrary", "arbitrary"),
            vmem_limit_bytes=VMEM_LIMIT_WIDE),
        name="inproj",
    )(hn, w_in, wqt, wvt, gq, gk, cq, sq, ct, st, gmat)


def _lam_values(lam_ref):
    lv = lam_ref[...]
    lam_init = lv[4:5, 0:1]
    lam = (jnp.exp(jnp.sum(lv[0:1] * lv[1:2], axis=-1, keepdims=True))
           - jnp.exp(jnp.sum(lv[2:3] * lv[3:4], axis=-1, keepdims=True)) + lam_init)
    return lam, lam_init


def _finish_heads(o1, o2, lam, lam_init, g_ref, ga_ref, o_ref):
    d = o1 - lam * o2
    dn = d * lax.rsqrt(jnp.mean(d * d, axis=0, keepdims=True) + EPS)
    dn = dn * g_ref[...] * (1.0 - lam_init)
    ga = ga_ref[...].astype(jnp.float32)
    o_ref[...] = (dn.T * (ga * _sigmoid(ga))).astype(o_ref.dtype)


def _split_maps(qt_ref):
    qt = qt_ref[...].astype(jnp.float32)
    row = lax.broadcasted_iota(jnp.int32, qt.shape, 0)
    in1 = row < QK
    return qt, row, in1, (jnp.where(in1, qt, 0.0), jnp.where(in1, 0.0, qt))


def _online_step(k, vt, qt, m, l, acc):
    st = jnp.dot(k, qt, preferred_element_type=jnp.float32)
    m_new = jnp.maximum(m, jnp.max(st, axis=0, keepdims=True))
    alpha = jnp.exp2(m - m_new)
    p = jnp.exp2(st - m_new)
    l_new = alpha * l + jnp.sum(p, axis=0, keepdims=True)
    acc_new = alpha * acc + jnp.dot(vt, p.astype(jnp.bfloat16), preferred_element_type=jnp.float32)
    return m_new, l_new, acc_new


def _online_init(tq):
    return (jnp.full((1, tq), -jnp.inf, jnp.float32), jnp.zeros((1, tq), jnp.float32),
            jnp.zeros((DV, tq), jnp.float32))


def _attn_ctx_kernel(lam_ref, qt_ref, k_ref, vt_ref, ga_ref, g_ref, ag_ref, o_ref):
    del ag_ref
    _, _, _, qms = _split_maps(qt_ref)
    k = k_ref[...]
    vt = vt_ref[0]
    outs = []
    for qm in qms:
        _, l, acc = _online_step(k, vt, qm.astype(jnp.bfloat16), *_online_init(CTX))
        outs.append(acc / l)
    _finish_heads(outs[0], outs[1], *_lam_values(lam_ref), g_ref, ga_ref, o_ref)


def _attn_lat_kernel(lam_ref, qt_ref, k_ref, vt_ref, ga_ref, g_ref, gmat_ref, o_ref, kmax_ref,
                     *, nchunks):
    i = pl.program_id(1)
    f32, bf = jnp.float32, jnp.bfloat16

    @pl.when(i == 0)
    def _():
        def kbody(c, mx):
            kk = k_ref[pl.ds(pl.multiple_of(c * TM, TM), TM), :].astype(f32)
            nrm = jnp.dot((kk * kk).astype(bf), gmat_ref[...], preferred_element_type=f32)
            return jnp.maximum(mx, jnp.max(nrm.reshape(TM // 8, 8, 2 * QK), axis=0))
        mx = lax.fori_loop(0, nchunks, kbody, jnp.zeros((8, 2 * QK), f32))
        kmax_ref[...] = jnp.sqrt(jnp.max(mx, axis=0, keepdims=True)) * BOUND_SLACK

    qt, row, in1, qms = _split_maps(qt_ref)
    qsq = qt * qt
    kmax = kmax_ref[...]
    shifts = (jnp.sqrt(jnp.sum(jnp.where(in1, qsq, 0.0), axis=0, keepdims=True)) * kmax[:, 0:1],
              jnp.sqrt(jnp.sum(jnp.where(in1, 0.0, qsq), axis=0, keepdims=True))
              * kmax[:, QK:QK + 1])
    qaug = tuple(jnp.concatenate([qm, jnp.where(row == 0, -sh, 0.0)], axis=0).astype(bf)
                 for qm, sh in zip(qms, shifts))
    shift_max = jnp.maximum(jnp.max(shifts[0]), jnp.max(shifts[1]))
    lam, lam_init = _lam_values(lam_ref)

    def key_rows(c):
        if isinstance(c, int):
            return k_ref[c * TM:(c + 1) * TM, :]
        return k_ref[pl.ds(pl.multiple_of(c * TM, TM), TM), :]

    @pl.when(shift_max <= SHIFT_SAFE)
    def _():
        ones_blk = jnp.ones((TM, V7X_LANES), bf)

        def body(c, stats):
            kaug = jnp.concatenate([key_rows(c), ones_blk], axis=1)
            vt = vt_ref[c]
            sts = [jnp.dot(kaug, qa, preferred_element_type=f32) for qa in qaug]
            ps = [jnp.exp2(st) for st in sts]
            out = []
            for p, (ls, acc) in zip(ps, stats):
                ls = ls + jnp.sum(p.reshape(TM // 8, 8, TQ), axis=0)
                acc = acc + jnp.dot(vt, p.astype(bf), preferred_element_type=f32)
                out.append((ls, acc))
            return tuple(out)

        init = tuple((jnp.zeros((8, TQ), f32), jnp.zeros((DV, TQ), f32)) for _ in range(2))
        stats = init
        for c in range(nchunks):
            stats = body(c, stats)
        (l1, a1), (l2, a2) = stats
        _finish_heads(a1 / jnp.sum(l1, axis=0, keepdims=True),
                      a2 / jnp.sum(l2, axis=0, keepdims=True), lam, lam_init, g_ref, ga_ref, o_ref)

    @pl.when(jnp.logical_not(shift_max <= SHIFT_SAFE))
    def _():
        def body(c, carry):
            k = key_rows(c)
            vt = vt_ref[c]
            return tuple(_online_step(k, vt, qa[0:2 * QK], *st) for qa, st in zip(qaug, carry))

        (_, l1, a1), (_, l2, a2) = lax.fori_loop(0, nchunks, body,
                                                 (_online_init(TQ), _online_init(TQ)))
        _finish_heads(a1 / l1, a2 / l2, lam, lam_init, g_ref, ga_ref, o_ref)


def _attn_call(lamv, z, qt, vt3, g_sub, gmat, with_ctx):
    nchunks = vt3.shape[0]
    lat = functools.partial(_attn_lat_kernel, nchunks=nchunks)
    ag = pl.pallas_call(
        lat,
        grid=(HEADS, SEQ // TQ),
        in_specs=[
            pl.BlockSpec((5, QK), lambda h, i: (0, 0)),
            pl.BlockSpec((2 * QK, TQ), lambda h, i: (h, i)),
            pl.BlockSpec((T, 2 * QK), lambda h, i: (0, ZC_K // DV + h)),
            pl.BlockSpec((nchunks, DV, TM), lambda h, i: (0, h, 0)),
            pl.BlockSpec((TQ, DV), lambda h, i: (i, ZC_GA // DV + h)),
            pl.BlockSpec((DV, 1), lambda h, i: (0, 0)),
            pl.BlockSpec((V7X_LANES, V7X_LANES), lambda h, i: (0, 0)),
        ],
        out_specs=pl.BlockSpec((TQ, DV), lambda h, i: (i, h)),
        out_shape=jax.ShapeDtypeStruct((T, AW), jnp.bfloat16),
        scratch_shapes=[pltpu.VMEM((1, 2 * QK), jnp.float32)],
        compiler_params=pltpu.CompilerParams(
            dimension_semantics=("arbitrary", "arbitrary"),
            vmem_limit_bytes=VMEM_LIMIT_WIDE),
        name="diff_attn",
    )(lamv, qt, z, vt3, z, g_sub, gmat)
    if not with_ctx:
        return ag
    cb = SEQ // CTX
    assert (SEQ % TM) % CTX == 0 and SEQ % TM + CTX == TM
    return pl.pallas_call(
        _attn_ctx_kernel,
        grid=(HEADS,),
        in_specs=[
            pl.BlockSpec((5, QK), lambda h: (0, 0)),
            pl.BlockSpec((2 * QK, CTX), lambda h: (h, cb)),
            pl.BlockSpec((CTX, 2 * QK), lambda h: (cb, ZC_K // DV + h)),
            pl.BlockSpec((1, DV, CTX), lambda h: (SEQ // TM, h, (SEQ % TM) // CTX)),
            pl.BlockSpec((CTX, DV), lambda h: (cb, ZC_GA // DV + h)),
            pl.BlockSpec((DV, 1), lambda h: (0, 0)),
            pl.BlockSpec(memory_space=pl.ANY),
        ],
        out_specs=pl.BlockSpec((CTX, DV), lambda h: (cb, h)),
        out_shape=jax.ShapeDtypeStruct((T, AW), jnp.bfloat16),
        input_output_aliases={6: 0},
        compiler_params=pltpu.CompilerParams(dimension_semantics=("arbitrary",)),
        name="ctx_attn",
    )(lamv, qt, z, vt3, z, g_sub, ag)


def _merge_kernel(*refs, first, has_next):
    n_res = 2 if first else 1
    (mod_ref, ag_ref, u_ref, v_ref, gb_ref, ma_ref, mb_ref, gv_ref, ws_ref, bs_ref, wpa_ref,
     wpb_ref, wo_ref) = refs[n_res:n_res + 13]
    rest = refs[n_res + 13:]
    if has_next:
        modn_ref, gn_ref, o_ref, hn_ref, bo_ref = rest
    else:
        o_ref, bo_ref = rest
    i = pl.program_id(0)
    is_ctx = i == SEQ // TB
    f32 = jnp.float32
    n_blocks = (TB // CHUNK) * GROUPS
    slab = D * GMLP_BLOCKS_PER_YA_SLAB // n_blocks
    ya_slabs = []
    for ck in range(TB // CHUNK):
        r0 = ck * CHUNK
        for g in range(GROUPS):
            idx = ck * GROUPS + g
            if idx % GMLP_BLOCKS_PER_YA_SLAB == 0:
                n0 = (idx // GMLP_BLOCKS_PER_YA_SLAB) * slab
                ya_slabs.append(jnp.dot(ag_ref[...], wpa_ref[0, :, n0:n0 + slab],
                                        preferred_element_type=f32))
            c0 = g * GDIM
            blk = (slice(r0, r0 + CHUNK), slice(c0, c0 + GDIM))
            v = _gelu_tanh(v_ref[blk].astype(f32))
            vn = v * lax.rsqrt(jnp.mean(v * v, axis=-1, keepdims=True) + EPS) * gv_ref[:, c0:c0 + GDIM]
            s = (jnp.dot(ws_ref[0, g], vn.astype(jnp.bfloat16), preferred_element_type=f32)
                 + bs_ref[0, g])
            gb = gb_ref[blk].astype(f32)
            b = _gelu_tanh(u_ref[blk].astype(f32)) * s * (gb * _sigmoid(gb))
            bo_ref[blk] = b.astype(bo_ref.dtype)
    ya = jnp.concatenate(ya_slabs, axis=1)
    yb = jnp.dot(bo_ref[...], wpb_ref[0], preferred_element_type=f32)
    mix = _sigmoid(ma_ref[...].astype(f32)) * ya + _sigmoid(mb_ref[...].astype(f32)) * yb
    out = jnp.dot(mix.astype(jnp.bfloat16), wo_ref[0], preferred_element_type=jnp.float32)
    gate = jnp.where(is_ctx, mod_ref[1:2, 2 * D:3 * D], mod_ref[0:1, 2 * D:3 * D])
    h = _stream_rows(refs[0], refs[1], is_ctx) if first else refs[0][...]
    h_new = h + gate * out
    o_ref[...] = h_new
    if has_next:
        hn_ref[...] = _adaln_norm(h_new, modn_ref, gn_ref, is_ctx)


def _merge_call(res, mod_l, ag, z, gv, ws, bs, wpa, wpb, wo, layer, nxt):
    first = len(res) == 2
    const = dict(pipeline_mode=pl.Buffered(1))
    rows = SEQ if nxt is None else T
    row_blk = pl.BlockSpec((TB, D), lambda i: (i, 0))
    res_specs = _stream_specs() if first else [row_blk]
    nxt_specs = [] if nxt is None else [pl.BlockSpec((2, 3 * D), lambda i: (0, 0)),
                                        pl.BlockSpec((1, D), lambda i: (0, 0))]
    h_out = jax.ShapeDtypeStruct((rows, D), jnp.float32)
    return pl.pallas_call(
        functools.partial(_merge_kernel, first=first, has_next=nxt is not None),
        grid=(rows // TB,),
        in_specs=res_specs + [
            pl.BlockSpec((2, 3 * D), lambda i: (0, 0)),
            pl.BlockSpec((TB, AW), lambda i: (i, 0)),
            pl.BlockSpec((TB, BW), lambda i: (i, ZC_U // BW)),
            pl.BlockSpec((TB, BW), lambda i: (i, ZC_VN // BW)),
            pl.BlockSpec((TB, BW), lambda i: (i, ZC_GB // BW)),
            pl.BlockSpec((TB, D), lambda i: (i, ZC_MG // D)),
            pl.BlockSpec((TB, D), lambda i: (i, ZC_MG // D + 1)),
            pl.BlockSpec((1, BW), lambda i: (0, 0)),
            pl.BlockSpec((1, GROUPS, CHUNK, CHUNK), lambda i: (layer, 0, 0, 0)),
            pl.BlockSpec((1, GROUPS, CHUNK, GDIM), lambda i: (layer, 0, 0, 0)),
            pl.BlockSpec((1, AW, D), lambda i: (layer, 0, 0), **const),
            pl.BlockSpec((1, BW, D), lambda i: (layer, 0, 0), **const),
            pl.BlockSpec((1, D, D), lambda i: (layer, 0, 0), **const),
        ] + nxt_specs,
        out_specs=row_blk if nxt is None else [row_blk, row_blk],
        out_shape=h_out if nxt is None else [h_out, jax.ShapeDtypeStruct((rows, D), jnp.bfloat16)],
        scratch_shapes=[pltpu.VMEM((TB, BW), jnp.bfloat16)],
        input_output_aliases={} if (first or nxt is None) else {0: 0},
        compiler_params=pltpu.CompilerParams(
            dimension_semantics=("arbitrary",),
            vmem_limit_bytes=VMEM_LIMIT_MERGE),
        name="gmlp_merge",
    )(*res, mod_l, ag, z, z, z, z, z, gv, ws, bs, wpa, wpb, wo, *(() if nxt is None else nxt))


def _rope_tables(seq):
    rows = seq // GRID_W
    row = jnp.repeat(jnp.arange(rows, dtype=jnp.int32), GRID_W).astype(jnp.float32)
    col = jnp.tile(jnp.arange(GRID_W, dtype=jnp.int32), rows).astype(jnp.float32)
    inv = ROPE_BASE ** (-jnp.arange(ROPE_F, dtype=jnp.float32) / ROPE_F)
    ang = jnp.concatenate([row[:, None] * inv, col[:, None] * inv], axis=1)
    cos = jnp.concatenate([jnp.cos(ang), jnp.ones((CTX, 2 * ROPE_F), jnp.float32)], axis=0)
    sin = jnp.concatenate([jnp.sin(ang), jnp.zeros((CTX, 2 * ROPE_F), jnp.float32)], axis=0)
    cr, cc = cos[:, :ROPE_F], cos[:, ROPE_F:]
    sr, sc = sin[:, :ROPE_F], sin[:, ROPE_F:]
    cq = jnp.tile(jnp.concatenate([cr, cr, cc, cc], axis=1), (1, 2))
    sq = jnp.tile(jnp.concatenate([-sr, sr, -sc, sc], axis=1), (1, 2))
    return cq, sq, cos.T, sin.T


def kernel(x, c, ctx, c_ctx, w_ada, b_ada, norm_g, w_in, q_norm_g, k_norm_g, lam_q1, lam_k1,
           lam_q2, lam_k2, subln_g, v_norm_g, w_spatial, b_spatial, w_proj_a, w_proj_b, w_out):
    assert x.shape == (1, SEQ, D) and ctx.shape == (1, CTX, D)
    assert T % TM == 0 and SEQ % TQ == 0 and TB == CTX

    bf = jnp.bfloat16
    cq, sq, ct, st = _rope_tables(SEQ)
    mod = _mod_call(jnp.stack([c[0], c_ctx], axis=1), w_ada, b_ada)
    gmat = (jnp.arange(V7X_LANES)[:, None] // QK == jnp.arange(V7X_LANES)[None, :] // QK).astype(bf)

    nq = HEADS * 2 * QK
    w_q, w_v = lax.optimization_barrier((w_in[:, :, :nq], w_in[:, :, 2 * nq:2 * nq + AW]))
    w_qt = jnp.swapaxes(w_q, 1, 2).astype(bf)
    w_vt = jnp.swapaxes(w_v, 1, 2).astype(bf)
    ws = w_spatial.astype(bf)
    bs = jnp.broadcast_to(b_spatial[..., None], b_spatial.shape + (GDIM,))
    wpa, wpb, wo = w_proj_a.astype(bf), w_proj_b.astype(bf), w_out.astype(bf)
    gk = jnp.tile(k_norm_g, (1, V7X_LANES // QK))
    lam_init = jnp.asarray([0.8 - 0.6 * math.exp(-0.3 * l) for l in range(DEPTH)], jnp.float32)
    lamv = jnp.stack([lam_q1, lam_k1, lam_q2, lam_k2,
                      jnp.broadcast_to(lam_init[:, None], lam_q1.shape)], axis=1)

    res = (x, ctx)
    hn = _prenorm_call(x, ctx, mod[0], norm_g[0][None, :])
    for l in range(DEPTH):
        z, qt, vt3 = _inproj_call(hn, w_in, l, w_qt, w_vt, q_norm_g[l][:, None],
                                  gk[l][None, :], cq, sq, ct, st, gmat)
        last = l == DEPTH - 1
        ag = _attn_call(lamv[l], z, qt, vt3, subln_g[l][:, None], gmat, with_ctx=not last)
        nxt = None if last else (mod[l + 1], norm_g[l + 1][None, :])
        out = _merge_call(res, mod[l], ag, z, v_norm_g[l].reshape(1, BW), ws, bs, wpa, wpb, wo, l,
                          nxt)
        h, hn = (out, None) if last else out
        res = (h,)
    return h[None]
```
